```python
import jax
import jax.numpy as jnp
from jax import lax
import numpy as np

D_MODEL = 4096
BATCH = 1
SEQ = 16384
DEPTH = 1

RWKV_HEAD = 64
RWKV_WIDTH = D_MODEL
RWKV_HEADS = RWKV_WIDTH // RWKV_HEAD
DECAY_RANK = max(32, round(1.8 * D_MODEL ** 0.5 / 32) * 32)
ICLR_RANK = max(32, round(1.8 * D_MODEL ** 0.5 / 32) * 32)
GATE_RANK = max(32, round(0.6 * D_MODEL ** 0.8 / 32) * 32)
GN_EPS = 64e-5

ATTN_HEAD = 128
ATTN_WIDTH = D_MODEL
ATTN_HEADS = ATTN_WIDTH // ATTN_HEAD
MOBA_BLOCK = 256
MOBA_TOPK = 3
MOBA_Q_CHUNK = 32
ROPE_THETA = 10000.0

FFN_HIDDEN = 256 * ((8 * D_MODEL + 3 * 256 - 1) // (3 * 256))

RWKV_IN = 3 * RWKV_WIDTH + DECAY_RANK + ICLR_RANK + GATE_RANK
ATTN_IN = 3 * ATTN_WIDTH
GATE_IN = 2 * D_MODEL
IN_WIDTH = RWKV_IN + ATTN_IN + GATE_IN

LN_EPS = 1e-5
DN_ALPHA = (2 * DEPTH) ** 0.25
DN_BETA = (8 * DEPTH) ** -0.25
NEG = -1e30

kernel_name = 'hybrid_rwkv7_moba_macaron_deepnorm'


def layer_norm(x, g, b):
    xf = x.astype(jnp.float32)
    mu = xf.mean(-1, keepdims=True)
    var = jnp.square(xf - mu).mean(-1, keepdims=True)
    return ((xf - mu) * lax.rsqrt(var + LN_EPS) * g + b).astype(x.dtype)


def swiglu(x, w_gate, w_up, w_down):
    return (jax.nn.silu(x @ w_gate) * (x @ w_up)) @ w_down


def rope(t, positions):
    dh = t.shape[-1]
    inv = ROPE_THETA ** (-jnp.arange(0, dh, 2, dtype=jnp.float32) / dh)
    ang = positions.astype(jnp.float32)[..., None] * inv
    cos = jnp.cos(ang)[:, :, None, :]
    sin = jnp.sin(ang)[:, :, None, :]
    tf = t.astype(jnp.float32)
    t1, t2 = tf[..., : dh // 2], tf[..., dh // 2:]
    return jnp.concatenate([t1 * cos - t2 * sin, t2 * cos + t1 * sin], -1).astype(t.dtype)


def wkv7_scan(r, w, k, v, a, b):
    bsz, _, h, n = r.shape

    def step(state, inp):
        r_t, w_t, k_t, v_t, a_t, b_t = inp
        sa = jnp.einsum('bhvk,bhk->bhv', state, a_t)
        state = (state * w_t[:, :, None, :] + sa[..., None] * b_t[:, :, None, :]
                 + v_t[..., None] * k_t[:, :, None, :])
        return state, jnp.einsum('bhvk,bhk->bhv', state, r_t)

    xs = tuple(jnp.swapaxes(t.astype(jnp.float32), 0, 1) for t in (r, w, k, v, a, b))
    _, y = lax.scan(step, jnp.zeros((bsz, h, n, n), jnp.float32), xs)
    return jnp.swapaxes(y, 0, 1)


def rwkv7_branch(p, shift_mix, decay_w0, decay_up, iclr_a0, iclr_up, gate_up,
                 k_k, k_a, r_k, lnx_w, lnx_b, w_o):
    bsz, s, _ = p.shape
    c = RWKV_WIDTH
    prev = jnp.pad(p, ((0, 0), (1, 0), (0, 0)))[:, :s]
    z = p + (prev - p) * shift_mix
    r, k, v = z[..., :c], z[..., c:2 * c], z[..., 2 * c:3 * c]
    o = 3 * c
    dw = z[..., o:o + DECAY_RANK]
    da = z[..., o + DECAY_RANK:o + DECAY_RANK + ICLR_RANK]
    dg = z[..., o + DECAY_RANK + ICLR_RANK:]

    w_log = -jax.nn.softplus(-(decay_w0 + (jnp.tanh(dw) @ decay_up).astype(jnp.float32))) - 0.5
    decay = jnp.exp(-jnp.exp(w_log))
    a = jax.nn.sigmoid(iclr_a0 + (da @ iclr_up).astype(jnp.float32))
    g = jax.nn.sigmoid(dg) @ gate_up

    def heads(t):
        return t.reshape(bsz, s, RWKV_HEADS, RWKV_HEAD)

    kk = heads((k * k_k).astype(jnp.float32))
    kk = kk / jnp.maximum(jnp.sqrt(jnp.sum(kk * kk, -1, keepdims=True)), 1e-12)
    k_mod = k.astype(jnp.float32) * (1.0 + (a - 1.0) * k_a)
    a_h = heads(a)
    y = wkv7_scan(heads(r), heads(decay), heads(k_mod), heads(v), -kk, kk * a_h)

    mu = y.mean(-1, keepdims=True)
    var = jnp.square(y - mu).mean(-1, keepdims=True)
    y = ((y - mu) * lax.rsqrt(var + GN_EPS)).reshape(bsz, s, c) * lnx_w + lnx_b
    bonus = jnp.sum(heads(r.astype(jnp.float32) * k_mod) * r_k, -1, keepdims=True) * heads(v)
    y = (y + bonus.reshape(bsz, s, c)).astype(p.dtype)
    return (y * g) @ w_o


def moba_branch(p, positions, w_o):
    bsz, s, _ = p.shape
    h, dh, blk, qc = ATTN_HEADS, ATTN_HEAD, MOBA_BLOCK, MOBA_Q_CHUNK
    q, k, v = jnp.split(p, 3, axis=-1)
    q = rope(q.reshape(bsz, s, h, dh), positions)
    k = rope(k.reshape(bsz, s, h, dh), positions)
    v = v.reshape(bsz, s, h, dh)
    nb = -(-s // blk)
    sp = nb * blk
    pad = ((0, 0), (0, sp - s), (0, 0), (0, 0))
    q, k, v = (jnp.pad(t, pad) for t in (q, k, v))
    kblk = k.reshape(bsz, nb, blk, h, dh)
    vblk = v.reshape(bsz, nb, blk, h, dh)
    n_sel = min(MOBA_TOPK, nb)
    scale = dh ** -0.5

    qblk_id = jnp.arange(sp) // blk
    kmean = kblk.astype(jnp.float32).mean(axis=2)
    gate = jnp.einsum('bshd,bnhd->bshn', q.astype(jnp.float32), kmean)
    past = jnp.arange(nb)[None, :] < qblk_id[:, None]
    gate = jnp.where(past[None, :, None, :], gate, NEG)
    _, sel = lax.top_k(gate, n_sel)
    sel_valid = jnp.arange(n_sel)[None, :] < jnp.minimum(qblk_id, n_sel)[:, None]

    kb_h = kblk.transpose(0, 3, 1, 2, 4)
    vb_h = vblk.transpose(0, 3, 1, 2, 4)
    nc = sp // qc
    q_chunks = q.reshape(bsz, nc, qc, h, dh).swapaxes(0, 1)
    sel_chunks = sel.reshape(bsz, nc, qc, h, n_sel).swapaxes(0, 1)
    valid_chunks = sel_valid.reshape(nc, qc, n_sel)
    bi = jnp.arange(bsz)[:, None, None, None]
    hi = jnp.arange(h)[None, None, :, None]

    def attend_chunk(args):
        q_c, sel_c, valid_c, ci = args
        start = ci * qc
        j = start // blk
        k_sel = kb_h[bi, hi, sel_c]
        v_sel = vb_h[bi, hi, sel_c]
        s_sel = jnp.einsum('bqhd,bqhtkd->bqhtk', q_c, k_sel,
                           preferred_element_type=jnp.float32) * scale
        s_sel = jnp.where(valid_c[None, :, None, :, None], s_sel, NEG)
        k_own = lax.dynamic_index_in_dim(kblk, j, axis=1, keepdims=False)
        v_own = lax.dynamic_index_in_dim(vblk, j, axis=1, keepdims=False)
        s_own = jnp.einsum('bqhd,bkhd->bqhk', q_c, k_own,
                           preferred_element_type=jnp.float32) * scale
        causal = (j * blk + jnp.arange(blk))[None, :] <= (start + jnp.arange(qc))[:, None]
        s_own = jnp.where(causal[None, :, None, :], s_own, NEG)
        scores = jnp.concatenate([s_sel.reshape(bsz, qc, h, n_sel * blk), s_own], -1)
        prob = jax.nn.softmax(scores, axis=-1)
        p_sel = prob[..., :n_sel * blk].reshape(bsz, qc, h, n_sel, blk).astype(v_sel.dtype)
        p_own = prob[..., n_sel * blk:].astype(v_own.dtype)
        out = (jnp.einsum('bqhtk,bqhtkd->bqhd', p_sel, v_sel, preferred_element_type=jnp.float32)
               + jnp.einsum('bqhk,bkhd->bqhd', p_own, v_own, preferred_element_type=jnp.float32))
        return out.astype(q_c.dtype)

    out = lax.map(attend_chunk, (q_chunks, sel_chunks, valid_chunks, jnp.arange(nc)))
    out = out.swapaxes(0, 1).reshape(bsz, sp, h * dh)[:, :s]
    return out @ w_o


def token_mixing(hdn, positions, w_in, shift_mix, decay_w0, decay_up, iclr_a0, iclr_up,
                 gate_up, k_k, k_a, r_k, lnx_w, lnx_b, w_o_rwkv, w_o_attn, w_out):
    proj = hdn @ w_in
    p_rwkv = proj[..., :RWKV_IN]
    p_attn = proj[..., RWKV_IN:RWKV_IN + ATTN_IN]
    p_gate = proj[..., RWKV_IN + ATTN_IN:]
    y_rwkv = rwkv7_branch(p_rwkv, shift_mix, decay_w0, decay_up, iclr_a0, iclr_up, gate_up,
                          k_k, k_a, r_k, lnx_w, lnx_b, w_o_rwkv)
    y_attn = moba_branch(p_attn, positions, w_o_attn)
    gates = jax.nn.sigmoid(p_gate)
    merged = gates[..., :D_MODEL] * y_rwkv + gates[..., D_MODEL:] * y_attn
    return merged @ w_out


def setup_inputs(seed: int = 0) -> dict:
    key = jax.random.key(seed)
    ks = iter(jax.random.split(key, 40))
    f32 = jnp.float32
    L = DEPTH

    def normal(shape, scale):
        return jax.random.normal(next(ks), shape, f32) * scale

    def near(shape, center, spread):
        return center + spread * jax.random.normal(next(ks), shape, f32)

    def uniform(shape, lo, hi):
        return jax.random.uniform(next(ks), shape, f32, lo, hi)

    d, f = D_MODEL, FFN_HIDDEN
    return {
        'x': normal((BATCH, SEQ, d), 1.0),
        'positions': jnp.broadcast_to(jnp.arange(SEQ, dtype=jnp.int32), (BATCH, SEQ)),
        'ffn1_w_gate': normal((L, d, f), d ** -0.5),
        'ffn1_w_up': normal((L, d, f), d ** -0.5),
        'ffn1_w_down': normal((L, f, d), f ** -0.5 * DN_BETA),
        'ln1_g': near((L, d), 1.0, 0.02),
        'ln1_b': normal((L, d), 0.02),
        'w_in': normal((L, d, IN_WIDTH), d ** -0.5),
        'shift_mix': uniform((L, RWKV_IN), 0.0, 1.0),
        'decay_w0': uniform((L, RWKV_WIDTH), -6.5, -1.5),
        'decay_up': normal((L, DECAY_RANK, RWKV_WIDTH), 0.5 * DECAY_RANK ** -0.5),
        'iclr_a0': normal((L, RWKV_WIDTH), 0.1),
        'iclr_up': normal((L, ICLR_RANK, RWKV_WIDTH), 0.5 * ICLR_RANK ** -0.5),
        'gate_up': normal((L, GATE_RANK, RWKV_WIDTH), GATE_RANK ** -0.5),
        'k_k': near((L, RWKV_WIDTH), 0.85, 0.02),
        'k_a': near((L, RWKV_WIDTH), 1.0, 0.02),
        'r_k': normal((L, RWKV_HEADS, RWKV_HEAD), 0.1),
        'lnx_w': near((L, RWKV_WIDTH), 1.0, 0.02),
        'lnx_b': normal((L, RWKV_WIDTH), 0.02),
        'w_o_rwkv': normal((L, RWKV_WIDTH, d), RWKV_WIDTH ** -0.5),
        'w_o_attn': normal((L, ATTN_WIDTH, d), ATTN_WIDTH ** -0.5),
        'w_out': normal((L, d, d), d ** -0.5 * DN_BETA),
        'ln2_g': near((L, d), 1.0, 0.02),
        'ln2_b': normal((L, d), 0.02),
        'ffn2_w_gate': normal((L, d, f), d ** -0.5),
        'ffn2_w_up': normal((L, d, f), d ** -0.5),
        'ffn2_w_down': normal((L, f, d), f ** -0.5 * DN_BETA),
        'ln3_g': near((L, d), 1.0, 0.02),
        'ln3_b': normal((L, d), 0.02),
    }


def reference(x, positions, ffn1_w_gate, ffn1_w_up, ffn1_w_down, ln1_g, ln1_b,
              w_in, shift_mix, decay_w0, decay_up, iclr_a0, iclr_up, gate_up,
              k_k, k_a, r_k, lnx_w, lnx_b, w_o_rwkv, w_o_attn, w_out, ln2_g, ln2_b,
              ffn2_w_gate, ffn2_w_up, ffn2_w_down, ln3_g, ln3_b):
    for l in range(DEPTH):
        x = layer_norm(DN_ALPHA * x + 0.5 * swiglu(x, ffn1_w_gate[l], ffn1_w_up[l], ffn1_w_down[l]),
                       ln1_g[l], ln1_b[l])
        mix = token_mixing(x, positions, w_in[l], shift_mix[l], decay_w0[l], decay_up[l],
                           iclr_a0[l], iclr_up[l], gate_up[l], k_k[l], k_a[l], r_k[l],
                           lnx_w[l], lnx_b[l], w_o_rwkv[l], w_o_attn[l], w_out[l])
        x = layer_norm(DN_ALPHA * x + mix, ln2_g[l], ln2_b[l])
        x = layer_norm(DN_ALPHA * x + 0.5 * swiglu(x, ffn2_w_gate[l], ffn2_w_up[l], ffn2_w_down[l]),
                       ln3_g[l], ln3_b[l])
    return x
```

```python
import functools

import jax
import jax.numpy as jnp
from jax import lax
from jax.experimental import pallas as pl
from jax.experimental.pallas import tpu as pltpu

F32 = jnp.float32
BF16 = jnp.bfloat16

LN_EPS = 1e-5
GN_EPS = 64e-5
RWKV_HEAD = 64
ATTN_HEAD = 128
MOBA_BLOCK = 256
MOBA_TOPK = 3
ROPE_THETA = 10000.0
NEG = -1e30

LANES = 128
SCAN_CHUNK = 64
VMEM_LIMIT = 56 * 1024 * 1024


def _cparams(sem):
    return pltpu.CompilerParams(dimension_semantics=sem, vmem_limit_bytes=VMEM_LIMIT)


def _pad_to(n, m):
    return (n + m - 1) // m * m


def _tile(n, pref):
    t = min(n, pref)
    assert n % t == 0, (n, t)
    return t


def _mm_kernel(*refs, na, nb, ne, dots, epilogue):
    a_refs = refs[:na]
    b_refs = refs[na:na + nb]
    e_refs = refs[na + nb:na + nb + ne]
    o_refs = refs[na + nb + ne:]
    accs = [jnp.dot(a_refs[i][...], b_refs[j][...], preferred_element_type=F32) for i, j in dots]
    outs = epilogue(accs, [e[...] for e in e_refs])
    for o, val in zip(o_refs, outs):
        o[...] = val.astype(o.dtype)


def _matmul(a_list, b_list, dots, epilogue, out_dtypes, *, tm, tn, extras=(), name):
    m = a_list[0].shape[0]
    n = b_list[0].shape[1]
    tm = _tile(m, tm)
    tn = _tile(n, tn)
    in_specs = [pl.BlockSpec((tm, a.shape[1]), lambda i, j: (i, 0)) for a in a_list]
    in_specs += [pl.BlockSpec((b.shape[0], tn), lambda i, j: (0, j)) for b in b_list]
    for arr, off in extras:
        if arr.shape[0] == 1:
            in_specs.append(pl.BlockSpec((1, tn), lambda i, j, off=off: (0, j + off)))
        else:
            in_specs.append(pl.BlockSpec((tm, tn), lambda i, j, off=off: (i, j + off)))
    kern = functools.partial(_mm_kernel, na=len(a_list), nb=len(b_list), ne=len(extras),
                             dots=dots, epilogue=epilogue)
    outs = pl.pallas_call(
        kern,
        grid=(m // tm, n // tn),
        in_specs=in_specs,
        out_specs=[pl.BlockSpec((tm, tn), lambda i, j: (i, j)) for _ in out_dtypes],
        out_shape=[jax.ShapeDtypeStruct((m, n), dt) for dt in out_dtypes],
        compiler_params=_cparams(("parallel", "arbitrary")),
        name=name,
    )(*a_list, *b_list, *[e for e, _ in extras])
    return outs


def _sigmoid(x):
    return 1.0 / (1.0 + jnp.exp(-x))


def _swiglu_epilogue(accs, _):
    g, u = accs
    return [g * _sigmoid(g) * u]


def _identity_epilogue(accs, _):
    return [accs[0]]


def _gated_merge_epilogue(accs, extras):
    yr, ya = accs
    pr, pa = extras
    return [_sigmoid(pr) * yr + _sigmoid(pa) * ya]


def _ln_kernel(x_ref, y_ref, g_ref, b_ref, o_ref, ob_ref, *, alpha, beta):
    z = alpha * x_ref[...] + beta * y_ref[...]
    mu = jnp.mean(z, axis=-1, keepdims=True)
    zc = z - mu
    var = jnp.mean(zc * zc, axis=-1, keepdims=True)
    out = zc * lax.rsqrt(var + LN_EPS) * g_ref[...] + b_ref[...]
    o_ref[...] = out
    ob_ref[...] = out.astype(BF16)


def _residual_ln(x, y, g, b, alpha, beta, name):
    s, d = x.shape
    tm = _tile(s, 256)
    row = pl.BlockSpec((tm, d), lambda i: (i, 0))
    vec = pl.BlockSpec((1, d), lambda i: (0, 0))
    return pl.pallas_call(
        functools.partial(_ln_kernel, alpha=alpha, beta=beta),
        grid=(s // tm,),
        in_specs=[row, row, vec, vec],
        out_specs=[row, row],
        out_shape=[jax.ShapeDtypeStruct((s, d), F32), jax.ShapeDtypeStruct((s, d), BF16)],
        compiler_params=_cparams(("parallel",)),
        name=name,
    )(x, y, g.reshape(1, d), b.reshape(1, d))


def _head_sum(x, ones_bd):
    parts = []
    for j in range(x.shape[1] // LANES):
        parts.append(jnp.dot(x[:, j * LANES:(j + 1) * LANES], ones_bd,
                             precision=lax.Precision.HIGHEST, preferred_element_type=F32))
    return parts[0] if len(parts) == 1 else jnp.concatenate(parts, axis=1)


def _shifted(cur, prev8, mix, first):
    rolled = pltpu.roll(cur, 1, axis=0)
    last = jnp.where(first, 0.0, prev8[7:8, :])
    row0 = lax.broadcasted_iota(jnp.int32, cur.shape, 0) == 0
    prev = jnp.where(row0, last, rolled)
    return cur + (prev - cur) * mix


def _rwkv_prep_kernel(pr_ref, pk_ref, pv_ref, pl_ref, pr8_ref, pk8_ref, pv8_ref, pl8_ref,
                      mr_ref, mk_ref, mv_ref, ml_ref,
                      w0_ref, a0_ref, kk_ref, ka_ref, rk_ref,
                      dup_ref, iup_ref, gup_ref,
                      r_out, lw_out, k_out, v_out, a_out, b_out, g_out, bonus_out,
                      *, decay_rank, iclr_rank):
    first = pl.program_id(0) == 0
    r = _shifted(pr_ref[...], pr8_ref[...], mr_ref[...], first)
    k = _shifted(pk_ref[...], pk8_ref[...], mk_ref[...], first)
    v = _shifted(pv_ref[...], pv8_ref[...], mv_ref[...], first)
    zl = _shifted(pl_ref[...], pl8_ref[...], ml_ref[...], first)
    dw = zl[:, :decay_rank]
    da = zl[:, decay_rank:decay_rank + iclr_rank]
    dg = zl[:, decay_rank + iclr_rank:]

    lw = jnp.dot(jnp.tanh(dw).astype(BF16), dup_ref[...], preferred_element_type=F32)
    la = jnp.dot(da.astype(BF16), iup_ref[...], preferred_element_type=F32)
    g = jnp.dot(_sigmoid(dg).astype(BF16), gup_ref[...], preferred_element_type=F32)

    u = -(w0_ref[...] + lw)
    softplus = jnp.maximum(u, 0.0) + jnp.log(1.0 + jnp.exp(-jnp.abs(u)))
    w_log = -softplus - 0.5
    log_decay = -jnp.exp(w_log)
    a_sig = _sigmoid(a0_ref[...] + la)

    lane = lax.broadcasted_iota(jnp.int32, (LANES, LANES), 0) // RWKV_HEAD
    lane_t = lax.broadcasted_iota(jnp.int32, (LANES, LANES), 1) // RWKV_HEAD
    ones_bd = (lane == lane_t).astype(F32)

    kk = k * kk_ref[...]
    norm = jnp.sqrt(_head_sum(kk * kk, ones_bd))
    kk = kk / jnp.maximum(norm, 1e-12)
    k_mod = k * (1.0 + (a_sig - 1.0) * ka_ref[...])
    bonus = _head_sum(r * k_mod * rk_ref[...], ones_bd) * v

    r_out[...] = r
    lw_out[...] = log_decay
    k_out[...] = k_mod
    v_out[...] = v
    a_out[...] = -kk
    b_out[...] = kk * a_sig
    g_out[...] = g
    bonus_out[...] = bonus


def _rwkv_prep(p, c, lora_w, shift_mix_p, decay_w0, iclr_a0, k_k, k_a, r_k_flat,
               decay_up, iclr_up, gate_up_p):
    s = p.shape[0]
    tm = _tile(s, 256)
    tc = _tile(c, 512)
    ncb = c // tc
    assert (3 * c) % lora_w == 0 and tm % 8 == 0
    lora_blk = 3 * c // lora_w
    decay_rank, iclr_rank = decay_up.shape[0], iclr_up.shape[0]

    def col(off):
        return pl.BlockSpec((tm, tc), lambda i, j: (i, j + off))

    def col8(off):
        return pl.BlockSpec((8, tc), lambda i, j: (jnp.maximum(i * (tm // 8) - 1, 0), j + off))

    def vec(off):
        return pl.BlockSpec((1, tc), lambda i, j: (0, j + off))

    in_specs = [
        col(0), col(ncb), col(2 * ncb),
        pl.BlockSpec((tm, lora_w), lambda i, j: (i, lora_blk)),
        col8(0), col8(ncb), col8(2 * ncb),
        pl.BlockSpec((8, lora_w), lambda i, j: (jnp.maximum(i * (tm // 8) - 1, 0), lora_blk)),
        vec(0), vec(ncb), vec(2 * ncb),
        pl.BlockSpec((1, lora_w), lambda i, j: (0, lora_blk)),
        vec(0), vec(0), vec(0), vec(0), vec(0),
        pl.BlockSpec((decay_rank, tc), lambda i, j: (0, j)),
        pl.BlockSpec((iclr_rank, tc), lambda i, j: (0, j)),
        pl.BlockSpec((gate_up_p.shape[0], tc), lambda i, j: (0, j)),
    ]
    out_spec = pl.BlockSpec((tm, tc), lambda i, j: (i, j))
    outs = pl.pallas_call(
        functools.partial(_rwkv_prep_kernel, decay_rank=decay_rank, iclr_rank=iclr_rank),
        grid=(s // tm, ncb),
        in_specs=in_specs,
        out_specs=[out_spec] * 8,
        out_shape=[jax.ShapeDtypeStruct((s, c), F32)] * 8,
        compiler_params=_cparams(("arbitrary", "arbitrary")),
        name="rwkv_prep",
    )(p, p, p, p, p, p, p, p,
      shift_mix_p, shift_mix_p, shift_mix_p, shift_mix_p,
      decay_w0, iclr_a0, k_k, k_a, r_k_flat,
      decay_up, iclr_up, gate_up_p)
    return outs


def _dot(a, b):
    return jnp.dot(a.astype(BF16), b.astype(BF16), preferred_element_type=F32)


def _dot_nt(a, b):
    return lax.dot_general(a.astype(BF16), b.astype(BF16), (((1,), (1,)), ((), ())),
                           preferred_element_type=F32)


def _dot_tn(a, b):
    return lax.dot_general(a.astype(BF16), b.astype(BF16), (((0,), (0,)), ((), ())),
                           preferred_element_type=F32)


def _unit_lower_inverse(a):
    n = a.shape[0]
    row = lax.broadcasted_iota(jnp.int32, (n, n), 0)
    col = lax.broadcasted_iota(jnp.int32, (n, n), 1)
    eye = (row == col).astype(F32)
    same16 = (row // 16) == (col // 16)
    same32 = (row // 32) == (col // 32)
    same64 = (row // 64) == (col // 64)
    a_d = jnp.where(same16, a, 0.0)
    a2 = _dot(a_d, a_d)
    a4 = _dot(a2, a2)
    a8 = _dot(a4, a4)
    lo = eye + a_d + a2 + _dot(a_d, a2)
    hi = eye + a4 + a8 + _dot(a4, a8)
    t = _dot(lo, hi)
    a_o1 = jnp.where(same32 & jnp.logical_not(same16), a, 0.0)
    t = t + _dot(_dot(t, a_o1), t)
    a_o2 = jnp.where(same64 & jnp.logical_not(same32), a, 0.0)
    t = t + _dot(_dot(t, a_o2), t)
    return t


def _scan_kernel(r_ref, lw_ref, k_ref, v_ref, a_ref, b_ref, y_ref, ht_ref, *, n_chunks, n_pairs):
    c = SCAN_CHUNK

    @pl.when(pl.program_id(1) == 0)
    def _():
        ht_ref[...] = jnp.zeros_like(ht_ref)

    row = lax.broadcasted_iota(jnp.int32, (2 * c, 2 * c), 0)
    col = lax.broadcasted_iota(jnp.int32, (2 * c, 2 * c), 1)
    strict = col < row
    incl = col <= row
    tri = (lax.broadcasted_iota(jnp.int32, (c, c), 1)
           <= lax.broadcasted_iota(jnp.int32, (c, c), 0)).astype(F32)
    head0 = lax.broadcasted_iota(jnp.int32, (c, LANES), 1) < RWKV_HEAD

    def stack(x):
        return jnp.concatenate([jnp.where(head0, x, 0.0), jnp.where(head0, 0.0, x)], axis=0)

    def chunk(ci, carry):
        rows = pl.ds(pl.multiple_of(ci * c, c), c)
        for g in range(n_pairs):
            lanes = slice(g * LANES, (g + 1) * LANES)
            lw = lw_ref[rows, lanes]
            cum = jnp.dot(tri, lw, precision=lax.Precision.HIGHEST, preferred_element_type=F32)
            total = cum[c - 1:c, :]
            p_in = jnp.exp(cum)
            p_ex = jnp.exp(cum - lw)
            p_inv = jnp.exp(-cum)
            p_rest = jnp.exp(total - cum)
            r = r_ref[rows, lanes]
            k = k_ref[rows, lanes]
            v = v_ref[rows, lanes]
            a = a_ref[rows, lanes]
            b = b_ref[rows, lanes]
            a_s = stack(a * p_ex)
            r_s = stack(r * p_in)
            b_s = stack(b * p_inv)
            k_s = stack(k * p_inv)
            bh_s = stack(b * p_rest)
            kh_s = stack(k * p_rest)
            v_s = stack(v)

            g_ab = jnp.where(strict, _dot_nt(a_s, b_s), 0.0)
            g_ak = jnp.where(strict, _dot_nt(a_s, k_s), 0.0)
            g_rb = jnp.where(incl, _dot_nt(r_s, b_s), 0.0)
            g_rk = jnp.where(incl, _dot_nt(r_s, k_s), 0.0)
            t = _unit_lower_inverse(g_ab)
            a_bar = _dot(t, a_s)
            v_bar = _dot(t, _dot(g_ak, v_s))
            kv = _dot_tn(v_s, kh_s)
            yv = _dot(g_rk, v_s)

            ht = ht_ref[g]
            u = _dot_nt(a_bar, ht) + v_bar
            y_s = _dot_nt(r_s, ht) + _dot(g_rb, u) + yv
            ht_ref[g] = ht * jnp.exp(total) + _dot_tn(u, bh_s) + kv
            y_ref[rows, lanes] = y_s[:c] + y_s[c:]
        return carry

    lax.fori_loop(0, n_chunks, chunk, 0)


def _rwkv_scan(r, lw, k, v, a, b):
    s, c = r.shape
    n_pairs = 2 if c % (2 * LANES) == 0 else 1
    wl = n_pairs * LANES
    tt = _tile(s, 512)
    spec = pl.BlockSpec((tt, wl), lambda p, t: (t, p))
    return pl.pallas_call(
        functools.partial(_scan_kernel, n_chunks=tt // SCAN_CHUNK, n_pairs=n_pairs),
        grid=(c // wl, s // tt),
        in_specs=[spec] * 6,
        out_specs=spec,
        out_shape=jax.ShapeDtypeStruct((s, c), F32),
        scratch_shapes=[pltpu.VMEM((n_pairs, LANES, LANES), F32)],
        compiler_params=_cparams(("parallel", "arbitrary")),
        name="rwkv_scan",
    )(r, lw, k, v, a, b)


def _rwkv_post_kernel(y_ref, bonus_ref, g_ref, w_ref, b_ref, o_ref):
    lane = lax.broadcasted_iota(jnp.int32, (LANES, LANES), 0) // RWKV_HEAD
    lane_t = lax.broadcasted_iota(jnp.int32, (LANES, LANES), 1) // RWKV_HEAD
    ones_bd = (lane == lane_t).astype(F32)
    y = y_ref[...]
    mu = _head_sum(y, ones_bd) * (1.0 / RWKV_HEAD)
    yc = y - mu
    var = _head_sum(yc * yc, ones_bd) * (1.0 / RWKV_HEAD)
    yn = yc * lax.rsqrt(var + GN_EPS) * w_ref[...] + b_ref[...]
    o_ref[...] = ((yn + bonus_ref[...]) * g_ref[...]).astype(o_ref.dtype)


def _rwkv_post(y, bonus, g, lnx_w, lnx_b):
    s, c = y.shape
    tm = _tile(s, 256)
    tc = _tile(c, 512)
    tile = pl.BlockSpec((tm, tc), lambda i, j: (i, j))
    vec = pl.BlockSpec((1, tc), lambda i, j: (0, j))
    return pl.pallas_call(
        _rwkv_post_kernel,
        grid=(s // tm, c // tc),
        in_specs=[tile, tile, tile, vec, vec],
        out_specs=tile,
        out_shape=jax.ShapeDtypeStruct((s, c), BF16),
        compiler_params=_cparams(("parallel", "arbitrary")),
        name="rwkv_post",
    )(y, bonus, g, lnx_w, lnx_b)


def _rope(t, pos, inv_ref, sign_ref):
    ang = pos * inv_ref[...]
    cos = jnp.cos(ang)
    sin = jnp.sin(ang) * sign_ref[...]
    return t * cos + pltpu.roll(t, ATTN_HEAD // 2, axis=1) * sin


def _kv_prep_kernel(pk_ref, pv_ref, pos_ref, inv_ref, sign_ref, k_out, v_out, kmean_out):
    pos = pos_ref[...]
    pk = pk_ref[...]
    parts = [_rope(pk[:, j * LANES:(j + 1) * LANES], pos, inv_ref, sign_ref)
             for j in range(pk.shape[1] // LANES)]
    kr = parts[0] if len(parts) == 1 else jnp.concatenate(parts, axis=1)
    k_out[...] = kr.astype(BF16)
    v_out[...] = pv_ref[...].astype(BF16)
    kmean_out[0] = jnp.mean(kr, axis=0, keepdims=True)


def _kv_prep(p, pos_col, inv, sign, k_off, v_off, width):
    s = p.shape[0]
    tw = 2 * LANES
    nb = s // MOBA_BLOCK
    return pl.pallas_call(
        _kv_prep_kernel,
        grid=(nb, width // tw),
        in_specs=[
            pl.BlockSpec((MOBA_BLOCK, tw), lambda i, j: (i, j + k_off)),
            pl.BlockSpec((MOBA_BLOCK, tw), lambda i, j: (i, j + v_off)),
            pl.BlockSpec((MOBA_BLOCK, 1), lambda i, j: (i, 0)),
            pl.BlockSpec((1, LANES), lambda i, j: (0, 0)),
            pl.BlockSpec((1, LANES), lambda i, j: (0, 0)),
        ],
        out_specs=[
            pl.BlockSpec((MOBA_BLOCK, tw), lambda i, j: (i, j)),
            pl.BlockSpec((MOBA_BLOCK, tw), lambda i, j: (i, j)),
            pl.BlockSpec((1, 1, tw), lambda i, j: (i, 0, j)),
        ],
        out_shape=[
            jax.ShapeDtypeStruct((s, width), BF16),
            jax.ShapeDtypeStruct((s, width), BF16),
            jax.ShapeDtypeStruct((nb, 1, width), F32),
        ],
        compiler_params=_cparams(("parallel", "arbitrary")),
        name="moba_kv_prep",
    )(p, p, pos_col, inv, sign)


def _q_prep_kernel(pq_ref, pos_ref, inv_ref, sign_ref, kmean_ref, q_out, *, nb):
    qb = pl.program_id(0)
    q = _rope(pq_ref[...], pos_ref[...], inv_ref, sign_ref)
    km = kmean_ref[:, 0, :]
    if nb < LANES:
        km = jnp.concatenate([km, jnp.zeros((LANES - nb, LANES), F32)], axis=0)
    gate = lax.dot_general(q, km, (((1,), (1,)), ((), ())),
                           precision=lax.Precision.HIGHEST, preferred_element_type=F32)
    blk = lax.broadcasted_iota(jnp.int32, gate.shape, 1)
    gate = jnp.where(blk < qb, gate, NEG)
    chosen = blk == qb
    for r in range(MOBA_TOPK):
        top = jnp.max(gate, axis=1, keepdims=True)
        first = jnp.min(jnp.where(gate == top, blk, LANES), axis=1, keepdims=True)
        pick = blk == first
        chosen = chosen | (pick & (r < qb))
        gate = jnp.where(pick, NEG, gate)
    bias = jnp.where(chosen, 0.0, NEG)
    q_out[0] = jnp.concatenate([q * (ATTN_HEAD ** -0.5), bias], axis=1).astype(BF16)


def _q_prep(p, pos_col, inv, sign, kmean, q_off, heads):
    s = p.shape[0]
    nb = s // MOBA_BLOCK
    assert nb <= LANES
    return pl.pallas_call(
        functools.partial(_q_prep_kernel, nb=nb),
        grid=(nb, heads),
        in_specs=[
            pl.BlockSpec((MOBA_BLOCK, LANES), lambda i, h: (i, h + q_off)),
            pl.BlockSpec((MOBA_BLOCK, 1), lambda i, h: (i, 0)),
            pl.BlockSpec((1, LANES), lambda i, h: (0, 0)),
            pl.BlockSpec((1, LANES), lambda i, h: (0, 0)),
            pl.BlockSpec((nb, 1, LANES), lambda i, h: (0, 0, h)),
        ],
        out_specs=pl.BlockSpec((1, MOBA_BLOCK, 2 * LANES), lambda i, h: (h, i, 0)),
        out_shape=jax.ShapeDtypeStruct((heads, s, 2 * LANES), BF16),
        compiler_params=_cparams(("parallel", "arbitrary")),
        name="moba_q_prep",
    )(p, pos_col, inv, sign, kmean)


def _attn_kernel(q_ref, k_ref, v_ref, o_ref, *, kt):
    qb = pl.program_id(1)
    q = q_ref[0]
    blocks_per_tile = kt // MOBA_BLOCK
    q_pos = qb * MOBA_BLOCK + lax.broadcasted_iota(jnp.int32, (MOBA_BLOCK, kt), 0)
    key_in_tile = lax.broadcasted_iota(jnp.int32, (MOBA_BLOCK, kt), 1)
    hot_row = lax.broadcasted_iota(jnp.int32, (kt, LANES), 0) // MOBA_BLOCK
    hot_lane = lax.broadcasted_iota(jnp.int32, (kt, LANES), 1)

    def body(g, carry):
        m, l, acc = carry
        rows = pl.ds(pl.multiple_of(g * kt, kt), kt)
        k = k_ref[rows, :]
        v = v_ref[rows, :]
        onehot = (hot_lane == hot_row + g * blocks_per_tile).astype(BF16)
        k_aug = jnp.concatenate([k, onehot], axis=1)
        s = lax.dot_general(q, k_aug, (((1,), (1,)), ((), ())), preferred_element_type=F32)
        s = jnp.where(key_in_tile + g * kt <= q_pos, s, NEG)
        m_new = jnp.maximum(m, jnp.max(s, axis=1, keepdims=True))
        alpha = jnp.exp(m - m_new)
        p = jnp.exp(s - m_new)
        l = alpha * l + jnp.sum(p, axis=1, keepdims=True)
        acc = alpha * acc + jnp.dot(p.astype(BF16), v, preferred_element_type=F32)
        return m_new, l, acc

    n_tiles = (qb * MOBA_BLOCK + MOBA_BLOCK + kt - 1) // kt
    init = (jnp.full((MOBA_BLOCK, 1), NEG, F32), jnp.zeros((MOBA_BLOCK, 1), F32),
            jnp.zeros((MOBA_BLOCK, ATTN_HEAD), F32))
    _, l, acc = lax.fori_loop(0, n_tiles, body, init)
    o_ref[...] = (acc / l).astype(o_ref.dtype)


def _moba_attention(q_aug, k, v):
    heads, s, _ = q_aug.shape
    kt = _tile(s, 2 * MOBA_BLOCK)
    return pl.pallas_call(
        functools.partial(_attn_kernel, kt=kt),
        grid=(heads, s // MOBA_BLOCK),
        in_specs=[
            pl.BlockSpec((1, MOBA_BLOCK, 2 * LANES), lambda h, i: (h, i, 0)),
            pl.BlockSpec((s, ATTN_HEAD), lambda h, i: (0, h)),
            pl.BlockSpec((s, ATTN_HEAD), lambda h, i: (0, h)),
        ],
        out_specs=pl.BlockSpec((MOBA_BLOCK, ATTN_HEAD), lambda h, i: (i, h)),
        out_shape=jax.ShapeDtypeStruct((s, heads * ATTN_HEAD), BF16),
        compiler_params=_cparams(("parallel", "arbitrary")),
        name="moba_attention",
    )(q_aug, k, v)


def _ffn_block(x, xb, w_gate, w_up, w_down, g, b, alpha, name):
    f = w_gate.shape[1]
    fp = _pad_to(f, 256)
    wg = jnp.pad(w_gate, ((0, 0), (0, fp - f))).astype(BF16)
    wu = jnp.pad(w_up, ((0, 0), (0, fp - f))).astype(BF16)
    wd = jnp.pad(w_down, ((0, fp - f), (0, 0))).astype(BF16)
    (h,) = _matmul([xb], [wg, wu], [(0, 0), (0, 1)], _swiglu_epilogue, [BF16],
                   tm=1024, tn=256, name=name + "_up")
    (y,) = _matmul([h], [wd], [(0, 0)], _identity_epilogue, [F32],
                   tm=512, tn=256, name=name + "_down")
    return _residual_ln(x, y, g, b, alpha, 0.5, name + "_ln")


def _token_mixing(x1b, positions, w_in, shift_mix, decay_w0, decay_up, iclr_a0, iclr_up, gate_up,
                  k_k, k_a, r_k, lnx_w, lnx_b, w_o_rwkv, w_o_attn, w_out):
    s, d = x1b.shape
    c = decay_w0.shape[0]
    gate_rank = gate_up.shape[0]
    decay_rank, iclr_rank = decay_up.shape[0], iclr_up.shape[0]
    rwkv_in = 3 * c + decay_rank + iclr_rank + gate_rank
    cw = w_o_attn.shape[0]
    heads = cw // ATTN_HEAD

    lora_w = _pad_to(decay_rank + iclr_rank + gate_rank, 256)
    pad = lora_w - (decay_rank + iclr_rank + gate_rank)
    w_in_p = jnp.concatenate(
        [w_in[:, :rwkv_in], jnp.zeros((d, pad), w_in.dtype), w_in[:, rwkv_in:]], axis=1).astype(BF16)
    mix_p = jnp.pad(shift_mix, (0, pad)).reshape(1, -1)
    gate_up_p = jnp.pad(gate_up, ((0, pad), (0, 0))).astype(BF16)
    rwkv_w = 3 * c + lora_w

    (p,) = _matmul([x1b], [w_in_p], [(0, 0)], _identity_epilogue, [F32],
                   tm=1024, tn=256, name="in_proj")

    r, lw, k_mod, v, a, b, g, bonus = _rwkv_prep(
        p, c, lora_w, mix_p, decay_w0.reshape(1, c), iclr_a0.reshape(1, c), k_k.reshape(1, c),
        k_a.reshape(1, c), r_k.reshape(1, c), decay_up.astype(BF16), iclr_up.astype(BF16), gate_up_p)
    y = _rwkv_scan(r, lw, k_mod, v, a, b)
    yg = _rwkv_post(y, bonus, g, lnx_w.reshape(1, c), lnx_b.reshape(1, c))

    half = ATTN_HEAD // 2
    inv_half = ROPE_THETA ** (-jnp.arange(0, ATTN_HEAD, 2, dtype=F32) / ATTN_HEAD)
    inv = jnp.concatenate([inv_half, inv_half]).reshape(1, ATTN_HEAD)
    sign = jnp.concatenate([-jnp.ones((half,), F32), jnp.ones((half,), F32)]).reshape(1, ATTN_HEAD)
    pos_col = positions.astype(F32).reshape(s, 1)
    assert rwkv_w % (2 * LANES) == 0 and cw % (2 * LANES) == 0
    k_rope, v_b, kmean = _kv_prep(p, pos_col, inv, sign, (rwkv_w + cw) // (2 * LANES),
                                  (rwkv_w + 2 * cw) // (2 * LANES), cw)
    q_aug = _q_prep(p, pos_col, inv, sign, kmean, rwkv_w // LANES, heads)
    attn = _moba_attention(q_aug, k_rope, v_b)

    gate_off = (rwkv_w + 3 * cw) // 256
    (merged,) = _matmul([yg, attn], [w_o_rwkv.astype(BF16), w_o_attn.astype(BF16)], [(0, 0), (1, 1)],
                        _gated_merge_epilogue, [BF16], tm=512, tn=256,
                        extras=[(p, gate_off), (p, gate_off + d // 256)], name="branch_out")
    (mix,) = _matmul([merged], [w_out.astype(BF16)], [(0, 0)], _identity_epilogue, [F32],
                     tm=1024, tn=256, name="mix_out")
    return mix


def kernel(x, positions, ffn1_w_gate, ffn1_w_up, ffn1_w_down, ln1_g, ln1_b, w_in, shift_mix, decay_w0, decay_up, iclr_a0, iclr_up, gate_up, k_k, k_a, r_k, lnx_w, lnx_b, w_o_rwkv, w_o_attn, w_out, ln2_g, ln2_b, ffn2_w_gate, ffn2_w_up, ffn2_w_down, ln3_g, ln3_b):
    bsz, s, d = x.shape
    depth = w_in.shape[0]
    alpha = (2 * depth) ** 0.25
    outs = []
    for bi in range(bsz):
        xf = x[bi]
        xb = xf.astype(BF16)
        for l in range(depth):
            xf, xb = _ffn_block(xf, xb, ffn1_w_gate[l], ffn1_w_up[l], ffn1_w_down[l],
                                ln1_g[l], ln1_b[l], alpha, "ffn1")
            mix = _token_mixing(xb, positions[bi], w_in[l], shift_mix[l], decay_w0[l], decay_up[l],
                                iclr_a0[l], iclr_up[l], gate_up[l], k_k[l], k_a[l], r_k[l],
                                lnx_w[l], lnx_b[l], w_o_rwkv[l], w_o_attn[l], w_out[l])
            xf, xb = _residual_ln(xf, mix, ln2_g[l], ln2_b[l], alpha, 1.0, "ln2")
            xf, xb = _ffn_block(xf, xb, ffn2_w_gate[l], ffn2_w_up[l], ffn2_w_down[l],
                                ln3_g[l], ln3_b[l], alpha, "ffn2")
        outs.append(xf)
    return jnp.stack(outs)
```

```python
import functools

import jax
import jax.numpy as jnp
from jax import lax
from jax.experimental import pallas as pl
from jax.experimental.pallas import tpu as pltpu

F32 = jnp.float32
BF16 = jnp.bfloat16

LN_EPS = 1e-5
GN_EPS = 64e-5
RWKV_HEAD = 64
ATTN_HEAD = 128
MOBA_BLOCK = 256
MOBA_TOPK = 3
ROPE_THETA = 10000.0
NEG = -1e30

SCORE_SCALE = ATTN_HEAD ** -0.5 * 1.4426950408889634
ATTN_KEY_TILE = 1024

VT_ROWS = ATTN_HEAD + 16

LANES = 128
SCAN_CHUNK = 64
VMEM_LIMIT = 56 * 1024 * 1024


def _cparams(sem):
    return pltpu.CompilerParams(dimension_semantics=sem, vmem_limit_bytes=VMEM_LIMIT)


def _pad_to(n, m):
    return (n + m - 1) // m * m


def _tile(n, pref):
    t = min(n, pref)
    assert n % t == 0, (n, t)
    return t


def _mm_kernel(*refs, na, nb, ne, dots, epilogue):
    a_refs = refs[:na]
    b_refs = refs[na:na + nb]
    e_refs = refs[na + nb:na + nb + ne]
    o_refs = refs[na + nb + ne:]
    accs = [jnp.dot(a_refs[i][...], b_refs[j][...], preferred_element_type=F32) for i, j in dots]
    outs = epilogue(accs, [e[...] for e in e_refs])
    for o, val in zip(o_refs, outs):
        o[...] = val.astype(o.dtype)


def _matmul(a_list, b_list, dots, epilogue, out_dtypes, *, tm, tn, extras=(), name):
    m = a_list[0].shape[0]
    n = b_list[0].shape[1]
    tm = _tile(m, tm)
    tn = _tile(n, tn)
    in_specs = [pl.BlockSpec((tm, a.shape[1]), lambda i, j: (i, 0)) for a in a_list]
    in_specs += [pl.BlockSpec((b.shape[0], tn), lambda i, j: (0, j)) for b in b_list]
    for arr, off in extras:
        if arr.shape[0] == 1:
            in_specs.append(pl.BlockSpec((1, tn), lambda i, j, off=off: (0, j + off)))
        else:
            in_specs.append(pl.BlockSpec((tm, tn), lambda i, j, off=off: (i, j + off)))
    kern = functools.partial(_mm_kernel, na=len(a_list), nb=len(b_list), ne=len(extras),
                             dots=dots, epilogue=epilogue)
    outs = pl.pallas_call(
        kern,
        grid=(m // tm, n // tn),
        in_specs=in_specs,
        out_specs=[pl.BlockSpec((tm, tn), lambda i, j: (i, j)) for _ in out_dtypes],
        out_shape=[jax.ShapeDtypeStruct((m, n), dt) for dt in out_dtypes],
        compiler_params=_cparams(("parallel", "arbitrary")),
        name=name,
    )(*a_list, *b_list, *[e for e, _ in extras])
    return outs


def _sigmoid(x):
    return 1.0 / (1.0 + jnp.exp(-x))


def _swiglu_epilogue(accs, _):
    g, u = accs
    return [g * _sigmoid(g) * u]


def _identity_epilogue(accs, _):
    return [accs[0]]


def _gated_merge_epilogue(accs, extras):
    yr, ya = accs
    pr, pa = extras
    return [_sigmoid(pr) * yr + _sigmoid(pa) * ya]


def _ln_kernel(x_ref, y_ref, g_ref, b_ref, o_ref, ob_ref, *, alpha, beta):
    z = alpha * x_ref[...] + beta * y_ref[...]
    mu = jnp.mean(z, axis=-1, keepdims=True)
    zc = z - mu
    var = jnp.mean(zc * zc, axis=-1, keepdims=True)
    out = zc * lax.rsqrt(var + LN_EPS) * g_ref[...] + b_ref[...]
    o_ref[...] = out
    ob_ref[...] = out.astype(BF16)


def _residual_ln(x, y, g, b, alpha, beta, name):
    s, d = x.shape
    tm = _tile(s, 256)
    row = pl.BlockSpec((tm, d), lambda i: (i, 0))
    vec = pl.BlockSpec((1, d), lambda i: (0, 0))
    return pl.pallas_call(
        functools.partial(_ln_kernel, alpha=alpha, beta=beta),
        grid=(s // tm,),
        in_specs=[row, row, vec, vec],
        out_specs=[row, row],
        out_shape=[jax.ShapeDtypeStruct((s, d), F32), jax.ShapeDtypeStruct((s, d), BF16)],
        compiler_params=_cparams(("parallel",)),
        name=name,
    )(x, y, g.reshape(1, d), b.reshape(1, d))


def _head_sum(x, ones_bd):
    parts = []
    for j in range(x.shape[1] // LANES):
        parts.append(jnp.dot(x[:, j * LANES:(j + 1) * LANES], ones_bd,
                             precision=lax.Precision.HIGHEST, preferred_element_type=F32))
    return parts[0] if len(parts) == 1 else jnp.concatenate(parts, axis=1)


def _shifted(cur, prev8, mix, first):
    rolled = pltpu.roll(cur, 1, axis=0)
    last = jnp.where(first, 0.0, prev8[7:8, :])
    row0 = lax.broadcasted_iota(jnp.int32, cur.shape, 0) == 0
    prev = jnp.where(row0, last, rolled)
    return cur + (prev - cur) * mix


def _rwkv_prep_kernel(pr_ref, pk_ref, pv_ref, pl_ref, pr8_ref, pk8_ref, pv8_ref, pl8_ref,
                      mr_ref, mk_ref, mv_ref, ml_ref,
                      w0_ref, a0_ref, kk_ref, ka_ref, rk_ref,
                      dup_ref, iup_ref, gup_ref,
                      r_out, lw_out, k_out, v_out, a_out, b_out, g_out, bonus_out,
                      *, decay_rank, iclr_rank):
    first = pl.program_id(0) == 0
    r = _shifted(pr_ref[...], pr8_ref[...], mr_ref[...], first)
    k = _shifted(pk_ref[...], pk8_ref[...], mk_ref[...], first)
    v = _shifted(pv_ref[...], pv8_ref[...], mv_ref[...], first)
    zl = _shifted(pl_ref[...], pl8_ref[...], ml_ref[...], first)
    dw = zl[:, :decay_rank]
    da = zl[:, decay_rank:decay_rank + iclr_rank]
    dg = zl[:, decay_rank + iclr_rank:]

    lw = jnp.dot(jnp.tanh(dw).astype(BF16), dup_ref[...], preferred_element_type=F32)
    la = jnp.dot(da.astype(BF16), iup_ref[...], preferred_element_type=F32)
    g = jnp.dot(_sigmoid(dg).astype(BF16), gup_ref[...], preferred_element_type=F32)

    u = -(w0_ref[...] + lw)
    softplus = jnp.maximum(u, 0.0) + jnp.log(1.0 + jnp.exp(-jnp.abs(u)))
    w_log = -softplus - 0.5
    log_decay = -jnp.exp(w_log)
    a_sig = _sigmoid(a0_ref[...] + la)

    lane = lax.broadcasted_iota(jnp.int32, (LANES, LANES), 0) // RWKV_HEAD
    lane_t = lax.broadcasted_iota(jnp.int32, (LANES, LANES), 1) // RWKV_HEAD
    ones_bd = (lane == lane_t).astype(F32)

    kk = k * kk_ref[...]
    norm = jnp.sqrt(_head_sum(kk * kk, ones_bd))
    kk = kk / jnp.maximum(norm, 1e-12)
    k_mod = k * (1.0 + (a_sig - 1.0) * ka_ref[...])
    bonus = _head_sum(r * k_mod * rk_ref[...], ones_bd) * v

    r_out[...] = r
    lw_out[...] = log_decay
    k_out[...] = k_mod
    v_out[...] = v
    a_out[...] = -kk
    b_out[...] = kk * a_sig
    g_out[...] = g
    bonus_out[...] = bonus


def _rwkv_prep(p, c, lora_w, shift_mix_p, decay_w0, iclr_a0, k_k, k_a, r_k_flat,
               decay_up, iclr_up, gate_up_p):
    s = p.shape[0]
    tm = _tile(s, 256)
    tc = _tile(c, 512)
    ncb = c // tc
    assert (3 * c) % lora_w == 0 and tm % 8 == 0
    lora_blk = 3 * c // lora_w
    decay_rank, iclr_rank = decay_up.shape[0], iclr_up.shape[0]

    def col(off):
        return pl.BlockSpec((tm, tc), lambda i, j: (i, j + off))

    def col8(off):
        return pl.BlockSpec((8, tc), lambda i, j: (jnp.maximum(i * (tm // 8) - 1, 0), j + off))

    def vec(off):
        return pl.BlockSpec((1, tc), lambda i, j: (0, j + off))

    in_specs = [
        col(0), col(ncb), col(2 * ncb),
        pl.BlockSpec((tm, lora_w), lambda i, j: (i, lora_blk)),
        col8(0), col8(ncb), col8(2 * ncb),
        pl.BlockSpec((8, lora_w), lambda i, j: (jnp.maximum(i * (tm // 8) - 1, 0), lora_blk)),
        vec(0), vec(ncb), vec(2 * ncb),
        pl.BlockSpec((1, lora_w), lambda i, j: (0, lora_blk)),
        vec(0), vec(0), vec(0), vec(0), vec(0),
        pl.BlockSpec((decay_rank, tc), lambda i, j: (0, j)),
        pl.BlockSpec((iclr_rank, tc), lambda i, j: (0, j)),
        pl.BlockSpec((gate_up_p.shape[0], tc), lambda i, j: (0, j)),
    ]
    out_spec = pl.BlockSpec((tm, tc), lambda i, j: (i, j))
    outs = pl.pallas_call(
        functools.partial(_rwkv_prep_kernel, decay_rank=decay_rank, iclr_rank=iclr_rank),
        grid=(s // tm, ncb),
        in_specs=in_specs,
        out_specs=[out_spec] * 8,
        out_shape=[jax.ShapeDtypeStruct((s, c), F32)] * 8,
        compiler_params=_cparams(("arbitrary", "arbitrary")),
        name="rwkv_prep",
    )(p, p, p, p, p, p, p, p,
      shift_mix_p, shift_mix_p, shift_mix_p, shift_mix_p,
      decay_w0, iclr_a0, k_k, k_a, r_k_flat,
      decay_up, iclr_up, gate_up_p)
    return outs


def _dot(a, b):
    return jnp.dot(a.astype(BF16), b.astype(BF16), preferred_element_type=F32)


def _dot_nt(a, b):
    return lax.dot_general(a.astype(BF16), b.astype(BF16), (((1,), (1,)), ((), ())),
                           preferred_element_type=F32)


def _dot_tn(a, b):
    return lax.dot_general(a.astype(BF16), b.astype(BF16), (((0,), (0,)), ((), ())),
                           preferred_element_type=F32)


def _each(fn, *lists):
    return [fn(*args) for args in zip(*lists)]


def _unit_lower_inverse(a_list):
    n = a_list[0].shape[0]
    row = lax.broadcasted_iota(jnp.int32, (n, n), 0)
    col = lax.broadcasted_iota(jnp.int32, (n, n), 1)
    eye = (row == col).astype(F32)
    same16 = (row // 16) == (col // 16)
    same32 = (row // 32) == (col // 32)
    same64 = (row // 64) == (col // 64)
    off32 = same32 & jnp.logical_not(same16)
    off64 = same64 & jnp.logical_not(same32)
    a_d = _each(lambda a: jnp.where(same16, a, 0.0).astype(BF16), a_list)
    a2 = _each(_dot, a_d, a_d)
    a4 = _each(_dot, a2, a2)
    a8 = _each(_dot, a4, a4)
    a3 = _each(_dot, a_d, a2)
    a12 = _each(_dot, a4, a8)
    lo = _each(lambda x1, x2, x3: eye + x1 + x2 + x3, a_d, a2, a3)
    hi = _each(lambda x4, x8, x12: eye + x4 + x8 + x12, a4, a8, a12)
    t = _each(_dot, lo, hi)
    for mask in (off32, off64):
        a_o = _each(lambda a: jnp.where(mask, a, 0.0).astype(BF16), a_list)
        tb = _each(lambda x: x.astype(BF16), t)
        ta = _each(_dot, tb, a_o)
        tat = _each(_dot, ta, tb)
        t = _each(lambda x, y: x + y, t, tat)
    return t


def _scan_kernel(r_ref, lw_ref, k_ref, v_ref, a_ref, b_ref, y_ref, ht_ref,
                 abar_ref, vbar_ref, vbart_ref, kv_ref, yv_ref, grb_ref, rs_ref, bhs_ref, decay_ref,
                 *, n_chunks, n_pairs, chunks_per_trip):
    c = SCAN_CHUNK

    @pl.when(pl.program_id(1) == 0)
    def _():
        ht_ref[...] = jnp.zeros_like(ht_ref)

    row = lax.broadcasted_iota(jnp.int32, (2 * c, 2 * c), 0)
    col = lax.broadcasted_iota(jnp.int32, (2 * c, 2 * c), 1)
    strict = col < row
    incl = col <= row
    tri = (lax.broadcasted_iota(jnp.int32, (c, c), 1)
           <= lax.broadcasted_iota(jnp.int32, (c, c), 0)).astype(F32)
    head0 = lax.broadcasted_iota(jnp.int32, (c, LANES), 1) < RWKV_HEAD

    def stack(x):
        return jnp.concatenate([jnp.where(head0, x, 0.0), jnp.where(head0, 0.0, x)],
                               axis=0).astype(BF16)

    def prepare(trip, carry):
        units = [(trip * chunks_per_trip + dc, g)
                 for dc in range(chunks_per_trip) for g in range(n_pairs)]

        def load(ref):
            return [ref[pl.ds(pl.multiple_of(ci * c, c), c), g * LANES:(g + 1) * LANES]
                    for ci, g in units]

        lw = load(lw_ref)
        cum = _each(lambda x: jnp.dot(tri, x, precision=lax.Precision.HIGHEST,
                                      preferred_element_type=F32), lw)
        total = _each(lambda x: x[c - 1:c, :], cum)
        p_in = _each(jnp.exp, cum)
        p_ex = _each(lambda x, y: jnp.exp(x - y), cum, lw)
        p_inv = _each(lambda x: jnp.exp(-x), cum)
        p_rest = _each(lambda t, x: jnp.exp(t - x), total, cum)
        r, k, v, a, b = load(r_ref), load(k_ref), load(v_ref), load(a_ref), load(b_ref)
        a_s = _each(lambda x, p: stack(x * p), a, p_ex)
        r_s = _each(lambda x, p: stack(x * p), r, p_in)
        b_s = _each(lambda x, p: stack(x * p), b, p_inv)
        k_s = _each(lambda x, p: stack(x * p), k, p_inv)
        bh_s = _each(lambda x, p: stack(x * p), b, p_rest)
        kh_s = _each(lambda x, p: stack(x * p), k, p_rest)
        v_s = _each(stack, v)

        g_ab = _each(lambda x, y: jnp.where(strict, _dot_nt(x, y), 0.0), a_s, b_s)
        g_ak = _each(lambda x, y: jnp.where(strict, _dot_nt(x, y), 0.0).astype(BF16), a_s, k_s)
        g_rb = _each(lambda x, y: jnp.where(incl, _dot_nt(x, y), 0.0).astype(BF16), r_s, b_s)
        g_rk = _each(lambda x, y: jnp.where(incl, _dot_nt(x, y), 0.0).astype(BF16), r_s, k_s)
        gv = _each(_dot, g_ak, v_s)
        kv = _each(_dot_tn, v_s, kh_s)
        yv = _each(_dot, g_rk, v_s)
        t = _each(lambda x: x.astype(BF16), _unit_lower_inverse(g_ab))
        a_bar = _each(_dot, t, a_s)
        v_bar = _each(_dot, t, gv)
        for i, (ci, g) in enumerate(units):
            abar_ref[ci, g] = a_bar[i].astype(BF16)
            vbar_ref[ci, g] = v_bar[i]
            vbart_ref[ci, g] = v_bar[i].T
            kv_ref[ci, g] = kv[i]
            yv_ref[ci, g] = yv[i]
            grb_ref[ci, g] = g_rb[i]
            rs_ref[ci, g] = r_s[i]
            bhs_ref[ci, g] = bh_s[i]
            decay_ref[ci, g] = jnp.exp(total[i])
        return carry

    lax.fori_loop(0, n_chunks // chunks_per_trip, prepare, 0)

    def advance(ci, carry):
        rows = pl.ds(pl.multiple_of(ci * c, c), c)
        pairs = list(range(n_pairs))
        ht = _each(lambda g: ht_ref[g], pairs)
        htb = _each(lambda x: x.astype(BF16), ht)
        a_bar = _each(lambda g: abar_ref[ci, g], pairs)
        u_t = _each(lambda h, ab, g: _dot_nt(h, ab) + vbart_ref[ci, g], htb, a_bar, pairs)
        u = _each(lambda h, ab, g: _dot_nt(ab, h) + vbar_ref[ci, g], htb, a_bar, pairs)
        upd = _each(lambda x, g: _dot(x, bhs_ref[ci, g]), u_t, pairs)
        for g in pairs:
            ht_ref[g] = ht[g] * decay_ref[ci, g] + upd[g] + kv_ref[ci, g]
        y_h = _each(lambda h, g: _dot_nt(rs_ref[ci, g], h), htb, pairs)
        y_u = _each(lambda x, g: _dot(grb_ref[ci, g], x), u, pairs)
        for g in pairs:
            y_s = y_h[g] + y_u[g] + yv_ref[ci, g]
            y_ref[rows, g * LANES:(g + 1) * LANES] = y_s[:c] + y_s[c:]
        return carry

    lax.fori_loop(0, n_chunks, advance, 0)


def _rwkv_scan(r, lw, k, v, a, b):
    s, c = r.shape
    n_pairs = 4 if c % (4 * LANES) == 0 else 1
    wl = n_pairs * LANES
    tt = _tile(s, 512)
    n_chunks = tt // SCAN_CHUNK
    spec = pl.BlockSpec((tt, wl), lambda p, t: (t, p))

    def per_unit(dtype, rows=LANES):
        return pltpu.VMEM((n_chunks, n_pairs, rows, LANES), dtype)

    return pl.pallas_call(
        functools.partial(_scan_kernel, n_chunks=n_chunks, n_pairs=n_pairs,
                          chunks_per_trip=2 if n_chunks % 2 == 0 else 1),
        grid=(c // wl, s // tt),
        in_specs=[spec] * 6,
        out_specs=spec,
        out_shape=jax.ShapeDtypeStruct((s, c), F32),
        scratch_shapes=[
            pltpu.VMEM((n_pairs, LANES, LANES), F32),
            per_unit(BF16), per_unit(F32), per_unit(F32), per_unit(F32), per_unit(F32),
            per_unit(BF16), per_unit(BF16), per_unit(BF16), per_unit(F32, rows=1),
        ],
        compiler_params=_cparams(("parallel", "arbitrary")),
        name="rwkv_scan",
    )(r, lw, k, v, a, b)


def _rwkv_post_kernel(y_ref, bonus_ref, g_ref, w_ref, b_ref, o_ref):
    lane = lax.broadcasted_iota(jnp.int32, (LANES, LANES), 0) // RWKV_HEAD
    lane_t = lax.broadcasted_iota(jnp.int32, (LANES, LANES), 1) // RWKV_HEAD
    ones_bd = (lane == lane_t).astype(F32)
    y = y_ref[...]
    mu = _head_sum(y, ones_bd) * (1.0 / RWKV_HEAD)
    yc = y - mu
    var = _head_sum(yc * yc, ones_bd) * (1.0 / RWKV_HEAD)
    yn = yc * lax.rsqrt(var + GN_EPS) * w_ref[...] + b_ref[...]
    o_ref[...] = ((yn + bonus_ref[...]) * g_ref[...]).astype(o_ref.dtype)


def _rwkv_post(y, bonus, g, lnx_w, lnx_b):
    s, c = y.shape
    tm = _tile(s, 256)
    tc = _tile(c, 512)
    tile = pl.BlockSpec((tm, tc), lambda i, j: (i, j))
    vec = pl.BlockSpec((1, tc), lambda i, j: (0, j))
    return pl.pallas_call(
        _rwkv_post_kernel,
        grid=(s // tm, c // tc),
        in_specs=[tile, tile, tile, vec, vec],
        out_specs=tile,
        out_shape=jax.ShapeDtypeStruct((s, c), BF16),
        compiler_params=_cparams(("parallel", "arbitrary")),
        name="rwkv_post",
    )(y, bonus, g, lnx_w, lnx_b)


def _rope_table_kernel(pos_ref, inv_ref, sign_ref, cos_out, sin_out):
    ang = pos_ref[...] * inv_ref[...]
    cos_out[...] = jnp.cos(ang)
    sin_out[...] = jnp.sin(ang) * sign_ref[...]


def _rope_tables(pos_col, inv, sign):
    s = pos_col.shape[0]
    tm = _tile(s, 512)
    vec = pl.BlockSpec((1, LANES), lambda i: (0, 0))
    tab = pl.BlockSpec((tm, LANES), lambda i: (i, 0))
    return pl.pallas_call(
        _rope_table_kernel,
        grid=(s // tm,),
        in_specs=[pl.BlockSpec((tm, 1), lambda i: (i, 0)), vec, vec],
        out_specs=[tab, tab],
        out_shape=[jax.ShapeDtypeStruct((s, LANES), F32)] * 2,
        compiler_params=_cparams(("parallel",)),
        name="rope_tables",
    )(pos_col, inv, sign)


def _rope(t, cos, sin):
    return t * cos + pltpu.roll(t, ATTN_HEAD // 2, axis=1) * sin


def _kv_prep_kernel(pk_ref, pv_ref, cos_ref, sin_ref, k_out, vt_out, kmean_out):
    cos = cos_ref[...]
    sin = sin_ref[...]
    pk = pk_ref[...]
    parts = [_rope(pk[:, j * LANES:(j + 1) * LANES], cos, sin) for j in range(pk.shape[1] // LANES)]
    kr = parts[0] if len(parts) == 1 else jnp.concatenate(parts, axis=1)
    k_out[...] = kr.astype(BF16)
    kmean_out[0] = jnp.mean(kr, axis=0, keepdims=True)
    pv = pv_ref[...]
    ones_rows = (lax.broadcasted_iota(jnp.int32, (VT_ROWS - ATTN_HEAD, MOBA_BLOCK), 0) == 0)
    for j in range(pv.shape[1] // LANES):
        vt_out[j, :ATTN_HEAD, :] = pv[:, j * LANES:(j + 1) * LANES].T.astype(BF16)
        vt_out[j, ATTN_HEAD:, :] = ones_rows.astype(BF16)


def _kv_prep(p, cos, sin, k_off, v_off, width):
    s = p.shape[0]
    tw = 2 * LANES
    nb = s // MOBA_BLOCK
    return pl.pallas_call(
        _kv_prep_kernel,
        grid=(nb, width // tw),
        in_specs=[
            pl.BlockSpec((MOBA_BLOCK, tw), lambda i, j: (i, j + k_off)),
            pl.BlockSpec((MOBA_BLOCK, tw), lambda i, j: (i, j + v_off)),
            pl.BlockSpec((MOBA_BLOCK, LANES), lambda i, j: (i, 0)),
            pl.BlockSpec((MOBA_BLOCK, LANES), lambda i, j: (i, 0)),
        ],
        out_specs=[
            pl.BlockSpec((MOBA_BLOCK, tw), lambda i, j: (i, j)),
            pl.BlockSpec((tw // LANES, VT_ROWS, MOBA_BLOCK), lambda i, j: (j, 0, i)),
            pl.BlockSpec((1, 1, tw), lambda i, j: (i, 0, j)),
        ],
        out_shape=[
            jax.ShapeDtypeStruct((s, width), BF16),
            jax.ShapeDtypeStruct((width // ATTN_HEAD, VT_ROWS, s), BF16),
            jax.ShapeDtypeStruct((nb, 1, width), F32),
        ],
        compiler_params=_cparams(("parallel", "arbitrary")),
        name="moba_kv_prep",
    )(p, p, cos, sin)


def _q_prep_kernel(pq_ref, cos_ref, sin_ref, kmean_ref, q_out, *, nb):
    qb = pl.program_id(0)
    q = _rope(pq_ref[...], cos_ref[...], sin_ref[...])
    km = kmean_ref[:, 0, :]
    if nb < LANES:
        km = jnp.concatenate([km, jnp.zeros((LANES - nb, LANES), F32)], axis=0)
    gate = lax.dot_general(q, km, (((1,), (1,)), ((), ())),
                           precision=lax.Precision.HIGHEST, preferred_element_type=F32)
    blk = lax.broadcasted_iota(jnp.int32, gate.shape, 1)
    gate = jnp.where(blk < qb, gate, NEG)
    chosen = blk == qb
    for r in range(MOBA_TOPK):
        top = jnp.max(gate, axis=1, keepdims=True)
        first = jnp.min(jnp.where(gate == top, blk, LANES), axis=1, keepdims=True)
        pick = blk == first
        chosen = chosen | (pick & (r < qb))
        gate = jnp.where(pick, NEG, gate)
    bias = jnp.where(chosen, 0.0, NEG)
    q_out[0] = jnp.concatenate([q * SCORE_SCALE, bias], axis=1).astype(BF16)


def _q_prep(p, cos, sin, kmean, q_off, heads):
    s = p.shape[0]
    nb = s // MOBA_BLOCK
    assert nb <= LANES
    tab = pl.BlockSpec((MOBA_BLOCK, LANES), lambda i, h: (i, 0))
    return pl.pallas_call(
        functools.partial(_q_prep_kernel, nb=nb),
        grid=(nb, heads),
        in_specs=[
            pl.BlockSpec((MOBA_BLOCK, LANES), lambda i, h: (i, h + q_off)),
            tab, tab,
            pl.BlockSpec((nb, 1, LANES), lambda i, h: (0, 0, h)),
        ],
        out_specs=pl.BlockSpec((1, MOBA_BLOCK, 2 * LANES), lambda i, h: (h, i, 0)),
        out_shape=jax.ShapeDtypeStruct((heads, s, 2 * LANES), BF16),
        compiler_params=_cparams(("parallel", "arbitrary")),
        name="moba_q_prep",
    )(p, cos, sin, kmean)


def _attn_kernel(q_ref, k_ref, hot_ref, vt_ref, o_ref,
                 s0_ref, s1_ref, mx0_ref, mx1_ref, m_ref, acc_ref, *, kt):
    qb = pl.program_id(1)
    q = q_ref[0]
    blocks_per_tile = kt // MOBA_BLOCK

    def scores(g, s_ref, mx_ref):
        rows = pl.ds(pl.multiple_of(g * kt, kt), kt)
        k_aug = jnp.concatenate([k_ref[rows, :], hot_ref[rows, :]], axis=1)
        s = lax.dot_general(k_aug, q, (((1,), (1,)), ((), ())), preferred_element_type=F32)
        s_ref[...] = s
        mx_ref[...] = jnp.max(s, axis=0, keepdims=True)

    def softmax_pv(g, s, s_max):
        m = m_ref[...]
        m_new = jnp.maximum(m, s_max)
        alpha = jnp.exp2(m - m_new)
        p = jnp.exp2((s - m_new).astype(BF16))
        vt = vt_ref[0, :, pl.ds(pl.multiple_of(g * kt, kt), kt)]
        acc_ref[...] = alpha * acc_ref[...] + jnp.dot(vt, p, preferred_element_type=F32)
        m_ref[...] = m_new

    n_past = qb // blocks_per_tile
    m_ref[...] = jnp.full(m_ref.shape, NEG, F32)
    acc_ref[...] = jnp.zeros(acc_ref.shape, F32)
    scores(0, s0_ref, mx0_ref)

    def pair(j, carry):
        g = 2 * j
        scores(g + 1, s1_ref, mx1_ref)
        softmax_pv(g, s0_ref[...], mx0_ref[...])
        scores(g + 2, s0_ref, mx0_ref)
        softmax_pv(g + 1, s1_ref[...], mx1_ref[...])
        return carry

    lax.fori_loop(0, n_past // 2, pair, 0)

    @pl.when(n_past % 2 == 1)
    def _():
        scores(n_past, s1_ref, mx1_ref)
        softmax_pv(n_past - 1, s0_ref[...], mx0_ref[...])
        s0_ref[...] = s1_ref[...]

    key_pos = n_past * kt + lax.broadcasted_iota(jnp.int32, (kt, MOBA_BLOCK), 0)
    q_pos = qb * MOBA_BLOCK + lax.broadcasted_iota(jnp.int32, (kt, MOBA_BLOCK), 1)
    s = jnp.where(key_pos <= q_pos, s0_ref[...], NEG)
    softmax_pv(n_past, s, jnp.max(s, axis=0, keepdims=True))
    acc = acc_ref[...]
    out_t = acc[:ATTN_HEAD] / acc[ATTN_HEAD:ATTN_HEAD + 1]
    o_ref[...] = out_t.T.astype(o_ref.dtype)


def _moba_attention(q_aug, k, vt):
    heads, s, _ = q_aug.shape
    kt = _tile(s, ATTN_KEY_TILE)
    block_onehot = (jnp.arange(s, dtype=jnp.int32)[:, None] // MOBA_BLOCK
                    == jnp.arange(LANES, dtype=jnp.int32)[None, :]).astype(BF16)
    return pl.pallas_call(
        functools.partial(_attn_kernel, kt=kt),
        grid=(heads, s // MOBA_BLOCK),
        in_specs=[
            pl.BlockSpec((1, MOBA_BLOCK, 2 * LANES), lambda h, i: (h, i, 0)),
            pl.BlockSpec((s, ATTN_HEAD), lambda h, i: (0, h)),
            pl.BlockSpec((s, LANES), lambda h, i: (0, 0)),
            pl.BlockSpec((1, VT_ROWS, s), lambda h, i: (h, 0, 0)),
        ],
        out_specs=pl.BlockSpec((MOBA_BLOCK, ATTN_HEAD), lambda h, i: (i, h)),
        out_shape=jax.ShapeDtypeStruct((s, heads * ATTN_HEAD), BF16),
        scratch_shapes=[
            pltpu.VMEM((kt, MOBA_BLOCK), F32), pltpu.VMEM((kt, MOBA_BLOCK), F32),
            pltpu.VMEM((1, MOBA_BLOCK), F32), pltpu.VMEM((1, MOBA_BLOCK), F32),
            pltpu.VMEM((1, MOBA_BLOCK), F32), pltpu.VMEM((VT_ROWS, MOBA_BLOCK), F32),
        ],
        compiler_params=_cparams(("parallel", "arbitrary")),
        name="moba_attention",
    )(q_aug, k, block_onehot, vt)


def _ffn_block(x, xb, w_gate, w_up, w_down, g, b, alpha, name):
    f = w_gate.shape[1]
    fp = _pad_to(f, 256)
    wg = jnp.pad(w_gate, ((0, 0), (0, fp - f))).astype(BF16)
    wu = jnp.pad(w_up, ((0, 0), (0, fp - f))).astype(BF16)
    wd = jnp.pad(w_down, ((0, fp - f), (0, 0))).astype(BF16)
    (h,) = _matmul([xb], [wg, wu], [(0, 0), (0, 1)], _swiglu_epilogue, [BF16],
                   tm=1024, tn=256, name=name + "_up")
    (y,) = _matmul([h], [wd], [(0, 0)], _identity_epilogue, [F32],
                   tm=512, tn=256, name=name + "_down")
    return _residual_ln(x, y, g, b, alpha, 0.5, name + "_ln")


def _token_mixing(x1b, positions, w_in, shift_mix, decay_w0, decay_up, iclr_a0, iclr_up, gate_up,
                  k_k, k_a, r_k, lnx_w, lnx_b, w_o_rwkv, w_o_attn, w_out):
    s, d = x1b.shape
    c = decay_w0.shape[0]
    gate_rank = gate_up.shape[0]
    decay_rank, iclr_rank = decay_up.shape[0], iclr_up.shape[0]
    rwkv_in = 3 * c + decay_rank + iclr_rank + gate_rank
    cw = w_o_attn.shape[0]
    heads = cw // ATTN_HEAD

    lora_w = _pad_to(decay_rank + iclr_rank + gate_rank, 256)
    pad = lora_w - (decay_rank + iclr_rank + gate_rank)
    w_in_p = jnp.concatenate(
        [w_in[:, :rwkv_in], jnp.zeros((d, pad), w_in.dtype), w_in[:, rwkv_in:]], axis=1).astype(BF16)
    mix_p = jnp.pad(shift_mix, (0, pad)).reshape(1, -1)
    gate_up_p = jnp.pad(gate_up, ((0, pad), (0, 0))).astype(BF16)
    rwkv_w = 3 * c + lora_w

    (p,) = _matmul([x1b], [w_in_p], [(0, 0)], _identity_epilogue, [F32],
                   tm=1024, tn=256, name="in_proj")

    r, lw, k_mod, v, a, b, g, bonus = _rwkv_prep(
        p, c, lora_w, mix_p, decay_w0.reshape(1, c), iclr_a0.reshape(1, c), k_k.reshape(1, c),
        k_a.reshape(1, c), r_k.reshape(1, c), decay_up.astype(BF16), iclr_up.astype(BF16), gate_up_p)
    y = _rwkv_scan(r, lw, k_mod, v, a, b)
    yg = _rwkv_post(y, bonus, g, lnx_w.reshape(1, c), lnx_b.reshape(1, c))

    half = ATTN_HEAD // 2
    inv_half = ROPE_THETA ** (-jnp.arange(0, ATTN_HEAD, 2, dtype=F32) / ATTN_HEAD)
    inv = jnp.concatenate([inv_half, inv_half]).reshape(1, ATTN_HEAD)
    sign = jnp.concatenate([-jnp.ones((half,), F32), jnp.ones((half,), F32)]).reshape(1, ATTN_HEAD)
    pos_col = positions.astype(F32).reshape(s, 1)
    assert rwkv_w % (2 * LANES) == 0 and cw % (2 * LANES) == 0
    cos, sin = _rope_tables(pos_col, inv, sign)
    k_rope, v_b, kmean = _kv_prep(p, cos, sin, (rwkv_w + cw) // (2 * LANES),
                                  (rwkv_w + 2 * cw) // (2 * LANES), cw)
    q_aug = _q_prep(p, cos, sin, kmean, rwkv_w // LANES, heads)
    attn = _moba_attention(q_aug, k_rope, v_b)

    gate_off = (rwkv_w + 3 * cw) // 256
    (merged,) = _matmul([yg, attn], [w_o_rwkv.astype(BF16), w_o_attn.astype(BF16)], [(0, 0), (1, 1)],
                        _gated_merge_epilogue, [BF16], tm=512, tn=256,
                        extras=[(p, gate_off), (p, gate_off + d // 256)], name="branch_out")
    (mix,) = _matmul([merged], [w_out.astype(BF16)], [(0, 0)], _identity_epilogue, [F32],
                     tm=1024, tn=256, name="mix_out")
    return mix


def kernel(x, positions, ffn1_w_gate, ffn1_w_up, ffn1_w_down, ln1_g, ln1_b, w_in, shift_mix, decay_w0, decay_up, iclr_a0, iclr_up, gate_up, k_k, k_a, r_k, lnx_w, lnx_b, w_o_rwkv, w_o_attn, w_out, ln2_g, ln2_b, ffn2_w_gate, ffn2_w_up, ffn2_w_down, ln3_g, ln3_b):
    bsz, s, d = x.shape
    depth = w_in.shape[0]
    alpha = (2 * depth) ** 0.25
    outs = []
    for bi in range(bsz):
        xf = x[bi]
        xb = xf.astype(BF16)
        for l in range(depth):
            xf, xb = _ffn_block(xf, xb, ffn1_w_gate[l], ffn1_w_up[l], ffn1_w_down[l],
                                ln1_g[l], ln1_b[l], alpha, "ffn1")
            mix = _token_mixing(xb, positions[bi], w_in[l], shift_mix[l], decay_w0[l], decay_up[l],
                                iclr_a0[l], iclr_up[l], gate_up[l], k_k[l], k_a[l], r_k[l],
                                lnx_w[l], lnx_b[l], w_o_rwkv[l], w_o_attn[l], w_out[l])
            xf, xb = _residual_ln(xf, mix, ln2_g[l], ln2_b[l], alpha, 1.0, "ln2")
            xf, xb = _ffn_block(xf, xb, ffn2_w_gate[l], ffn2_w_up[l], ffn2_w_down[l],
                                ln3_g[l], ln3_b[l], alpha, "ffn2")
        outs.append(xf)
    return jnp.stack(outs)
```

```python
import functools

import jax
import jax.numpy as jnp
from jax import lax
from jax.experimental import pallas as pl
from jax.experimental.pallas import tpu as pltpu

F32 = jnp.float32
BF16 = jnp.bfloat16

LN_EPS = 1e-5
GN_EPS = 64e-5
RWKV_HEAD = 64
ATTN_HEAD = 128
MOBA_BLOCK = 256
MOBA_TOPK = 3
ROPE_THETA = 10000.0
NEG = -1e30

SCORE_SCALE = ATTN_HEAD ** -0.5 * 1.4426950408889634
ATTN_KEY_TILE = 1024

VT_ROWS = ATTN_HEAD + 16

LANES = 128
SCAN_CHUNK = 64
VMEM_LIMIT = 56 * 1024 * 1024


def _cparams(sem):
    return pltpu.CompilerParams(dimension_semantics=sem, vmem_limit_bytes=VMEM_LIMIT)


def _pad_to(n, m):
    return (n + m - 1) // m * m


def _tile(n, pref):
    t = min(n, pref)
    assert n % t == 0, (n, t)
    return t


def _mm_kernel(*refs, na, nb, ne, dots, epilogue):
    a_refs = refs[:na]
    b_refs = refs[na:na + nb]
    e_refs = refs[na + nb:na + nb + ne]
    o_refs = refs[na + nb + ne:]
    accs = [jnp.dot(a_refs[i][...], b_refs[j][...], preferred_element_type=F32) for i, j in dots]
    outs = epilogue(accs, [e[...] for e in e_refs])
    for o, val in zip(o_refs, outs):
        o[...] = val.astype(o.dtype)


def _matmul(a_list, b_list, dots, epilogue, out_dtypes, *, tm, tn, extras=(), name):
    m = a_list[0].shape[0]
    n = b_list[0].shape[1]
    tm = _tile(m, tm)
    tn = _tile(n, tn)
    in_specs = [pl.BlockSpec((tm, a.shape[1]), lambda i, j: (i, 0)) for a in a_list]
    in_specs += [pl.BlockSpec((b.shape[0], tn), lambda i, j: (0, j)) for b in b_list]
    for arr, off in extras:
        if arr.shape[0] == 1:
            in_specs.append(pl.BlockSpec((1, tn), lambda i, j, off=off: (0, j + off)))
        else:
            in_specs.append(pl.BlockSpec((tm, tn), lambda i, j, off=off: (i, j + off)))
    kern = functools.partial(_mm_kernel, na=len(a_list), nb=len(b_list), ne=len(extras),
                             dots=dots, epilogue=epilogue)
    outs = pl.pallas_call(
        kern,
        grid=(m // tm, n // tn),
        in_specs=in_specs,
        out_specs=[pl.BlockSpec((tm, tn), lambda i, j: (i, j)) for _ in out_dtypes],
        out_shape=[jax.ShapeDtypeStruct((m, n), dt) for dt in out_dtypes],
        compiler_params=_cparams(("parallel", "arbitrary")),
        name=name,
    )(*a_list, *b_list, *[e for e, _ in extras])
    return outs


def _sigmoid(x):
    return 1.0 / (1.0 + jnp.exp(-x))


def _swiglu_epilogue(accs, _):
    g, u = accs
    return [g * _sigmoid(g) * u]


def _identity_epilogue(accs, _):
    return [accs[0]]


def _gated_merge_epilogue(accs, extras):
    yr, ya = accs
    pr, pa = extras
    return [_sigmoid(pr) * yr + _sigmoid(pa) * ya]


def _ln_kernel(x_ref, y_ref, g_ref, b_ref, o_ref, ob_ref, *, alpha, beta):
    z = alpha * x_ref[...] + beta * y_ref[...]
    mu = jnp.mean(z, axis=-1, keepdims=True)
    zc = z - mu
    var = jnp.mean(zc * zc, axis=-1, keepdims=True)
    out = zc * lax.rsqrt(var + LN_EPS) * g_ref[...] + b_ref[...]
    o_ref[...] = out
    ob_ref[...] = out.astype(BF16)


def _residual_ln(x, y, g, b, alpha, beta, name):
    s, d = x.shape
    tm = _tile(s, 256)
    row = pl.BlockSpec((tm, d), lambda i: (i, 0))
    vec = pl.BlockSpec((1, d), lambda i: (0, 0))
    return pl.pallas_call(
        functools.partial(_ln_kernel, alpha=alpha, beta=beta),
        grid=(s // tm,),
        in_specs=[row, row, vec, vec],
        out_specs=[row, row],
        out_shape=[jax.ShapeDtypeStruct((s, d), F32), jax.ShapeDtypeStruct((s, d), BF16)],
        compiler_params=_cparams(("parallel",)),
        name=name,
    )(x, y, g.reshape(1, d), b.reshape(1, d))


def _head_sum(x, ones_bd):
    hi = x.astype(BF16)
    rest = x - hi.astype(F32)
    mid = rest.astype(BF16)
    lo = (rest - mid.astype(F32)).astype(BF16)
    parts = []
    for j in range(x.shape[1] // LANES):
        lanes = slice(j * LANES, (j + 1) * LANES)
        parts.append(jnp.dot(hi[:, lanes], ones_bd, preferred_element_type=F32)
                     + jnp.dot(mid[:, lanes], ones_bd, preferred_element_type=F32)
                     + jnp.dot(lo[:, lanes], ones_bd, preferred_element_type=F32))
    return parts[0] if len(parts) == 1 else jnp.concatenate(parts, axis=1)


def _shifted(cur, prev8, mix, first):
    rolled = pltpu.roll(cur, 1, axis=0)
    last = jnp.where(first, 0.0, prev8[7:8, :])
    row0 = lax.broadcasted_iota(jnp.int32, cur.shape, 0) == 0
    prev = jnp.where(row0, last, rolled)
    return cur + (prev - cur) * mix


def _rwkv_prep_kernel(pr_ref, pk_ref, pv_ref, pl_ref, pr8_ref, pk8_ref, pv8_ref, pl8_ref,
                      mr_ref, mk_ref, mv_ref, ml_ref,
                      w0_ref, a0_ref, kk_ref, ka_ref, rk_ref,
                      dup_ref, iup_ref, gup_ref,
                      r_out, lw_out, k_out, v_out, a_out, b_out, g_out, bonus_out,
                      *, decay_rank, iclr_rank):
    first = pl.program_id(0) == 0
    r = _shifted(pr_ref[...], pr8_ref[...], mr_ref[...], first)
    k = _shifted(pk_ref[...], pk8_ref[...], mk_ref[...], first)
    v = _shifted(pv_ref[...], pv8_ref[...], mv_ref[...], first)
    zl = _shifted(pl_ref[...], pl8_ref[...], ml_ref[...], first)
    dw = zl[:, :decay_rank]
    da = zl[:, decay_rank:decay_rank + iclr_rank]
    dg = zl[:, decay_rank + iclr_rank:]

    lw = jnp.dot(jnp.tanh(dw).astype(BF16), dup_ref[...], preferred_element_type=F32)
    la = jnp.dot(da.astype(BF16), iup_ref[...], preferred_element_type=F32)
    g = jnp.dot(_sigmoid(dg).astype(BF16), gup_ref[...], preferred_element_type=F32)

    u = -(w0_ref[...] + lw)
    softplus = jnp.maximum(u, 0.0) + jnp.log(1.0 + jnp.exp(-jnp.abs(u)))
    w_log = -softplus - 0.5
    log_decay = -jnp.exp(w_log)
    a_sig = _sigmoid(a0_ref[...] + la)

    lane = lax.broadcasted_iota(jnp.int32, (LANES, LANES), 0) // RWKV_HEAD
    lane_t = lax.broadcasted_iota(jnp.int32, (LANES, LANES), 1) // RWKV_HEAD
    ones_bd = (lane == lane_t).astype(BF16)

    kk = k * kk_ref[...]
    norm = jnp.sqrt(_head_sum(kk * kk, ones_bd))
    kk = kk / jnp.maximum(norm, 1e-12)
    k_mod = k * (1.0 + (a_sig - 1.0) * ka_ref[...])
    bonus = _head_sum(r * k_mod * rk_ref[...], ones_bd) * v

    r_out[...] = r.astype(r_out.dtype)
    lw_out[...] = log_decay
    k_out[...] = k_mod.astype(k_out.dtype)
    v_out[...] = v.astype(v_out.dtype)
    a_out[...] = (-kk).astype(a_out.dtype)
    b_out[...] = (kk * a_sig).astype(b_out.dtype)
    g_out[...] = g.astype(g_out.dtype)
    bonus_out[...] = bonus.astype(bonus_out.dtype)


def _rwkv_prep(p, c, lora_w, shift_mix_p, decay_w0, iclr_a0, k_k, k_a, r_k_flat,
               decay_up, iclr_up, gate_up_p):
    s = p.shape[0]
    tm = _tile(s, 256)
    tc = _tile(c, 512)
    ncb = c // tc
    assert (3 * c) % lora_w == 0 and tm % 8 == 0
    lora_blk = 3 * c // lora_w
    decay_rank, iclr_rank = decay_up.shape[0], iclr_up.shape[0]

    def col(off):
        return pl.BlockSpec((tm, tc), lambda i, j: (i, j + off))

    def col8(off):
        return pl.BlockSpec((8, tc), lambda i, j: (jnp.maximum(i * (tm // 8) - 1, 0), j + off))

    def vec(off):
        return pl.BlockSpec((1, tc), lambda i, j: (0, j + off))

    in_specs = [
        col(0), col(ncb), col(2 * ncb),
        pl.BlockSpec((tm, lora_w), lambda i, j: (i, lora_blk)),
        col8(0), col8(ncb), col8(2 * ncb),
        pl.BlockSpec((8, lora_w), lambda i, j: (jnp.maximum(i * (tm // 8) - 1, 0), lora_blk)),
        vec(0), vec(ncb), vec(2 * ncb),
        pl.BlockSpec((1, lora_w), lambda i, j: (0, lora_blk)),
        vec(0), vec(0), vec(0), vec(0), vec(0),
        pl.BlockSpec((decay_rank, tc), lambda i, j: (0, j)),
        pl.BlockSpec((iclr_rank, tc), lambda i, j: (0, j)),
        pl.BlockSpec((gate_up_p.shape[0], tc), lambda i, j: (0, j)),
    ]
    out_spec = pl.BlockSpec((tm, tc), lambda i, j: (i, j))
    outs = pl.pallas_call(
        functools.partial(_rwkv_prep_kernel, decay_rank=decay_rank, iclr_rank=iclr_rank),
        grid=(s // tm, ncb),
        in_specs=in_specs,
        out_specs=[out_spec] * 8,
        out_shape=[jax.ShapeDtypeStruct((s, c), F32 if i == 1 else BF16) for i in range(8)],
        compiler_params=_cparams(("arbitrary", "arbitrary")),
        name="rwkv_prep",
    )(p, p, p, p, p, p, p, p,
      shift_mix_p, shift_mix_p, shift_mix_p, shift_mix_p,
      decay_w0, iclr_a0, k_k, k_a, r_k_flat,
      decay_up, iclr_up, gate_up_p)
    return outs


def _dot(a, b):
    return jnp.dot(a.astype(BF16), b.astype(BF16), preferred_element_type=F32)


def _dot_nt(a, b):
    return lax.dot_general(a.astype(BF16), b.astype(BF16), (((1,), (1,)), ((), ())),
                           preferred_element_type=F32)


def _dot_tn(a, b):
    return lax.dot_general(a.astype(BF16), b.astype(BF16), (((0,), (0,)), ((), ())),
                           preferred_element_type=F32)


def _each(fn, *lists):
    return [fn(*args) for args in zip(*lists)]


def _unit_lower_inverse(a_list):
    n = a_list[0].shape[0]
    row = lax.broadcasted_iota(jnp.int32, (n, n), 0)
    col = lax.broadcasted_iota(jnp.int32, (n, n), 1)
    eye = (row == col).astype(F32)
    same16 = (row // 16) == (col // 16)
    same32 = (row // 32) == (col // 32)
    same64 = (row // 64) == (col // 64)
    off32 = same32 & jnp.logical_not(same16)
    off64 = same64 & jnp.logical_not(same32)
    a_d = _each(lambda a: jnp.where(same16, a, 0.0).astype(BF16), a_list)
    a2 = _each(_dot, a_d, a_d)
    a4 = _each(_dot, a2, a2)
    a8 = _each(_dot, a4, a4)
    a3 = _each(_dot, a_d, a2)
    a12 = _each(_dot, a4, a8)
    lo = _each(lambda x1, x2, x3: eye + x1 + x2 + x3, a_d, a2, a3)
    hi = _each(lambda x4, x8, x12: eye + x4 + x8 + x12, a4, a8, a12)
    t = _each(_dot, lo, hi)
    for mask in (off32, off64):
        a_o = _each(lambda a: jnp.where(mask, a, 0.0).astype(BF16), a_list)
        tb = _each(lambda x: x.astype(BF16), t)
        ta = _each(_dot, tb, a_o)
        tat = _each(_dot, ta, tb)
        t = _each(lambda x, y: x + y, t, tat)
    return t


def _scan_kernel(r_ref, lw_ref, k_ref, v_ref, a_ref, b_ref, y_ref, ht_ref,
                 abar_ref, vbar_ref, vbart_ref, kv_ref, yv_ref, grb_ref, rs_ref, bhs_ref, decay_ref,
                 *, n_chunks, n_pairs, chunks_per_trip):
    c = SCAN_CHUNK

    @pl.when(pl.program_id(1) == 0)
    def _():
        ht_ref[...] = jnp.zeros_like(ht_ref)

    row = lax.broadcasted_iota(jnp.int32, (2 * c, 2 * c), 0)
    col = lax.broadcasted_iota(jnp.int32, (2 * c, 2 * c), 1)
    strict = col < row
    incl = col <= row
    tri = (lax.broadcasted_iota(jnp.int32, (c, c), 1)
           <= lax.broadcasted_iota(jnp.int32, (c, c), 0)).astype(F32)
    head0 = lax.broadcasted_iota(jnp.int32, (c, LANES), 1) < RWKV_HEAD

    def stack(x):
        return jnp.concatenate([jnp.where(head0, x, 0.0), jnp.where(head0, 0.0, x)],
                               axis=0).astype(BF16)

    def prepare(trip, carry):
        units = [(trip * chunks_per_trip + dc, g)
                 for dc in range(chunks_per_trip) for g in range(n_pairs)]

        def load(ref):
            return [ref[pl.ds(pl.multiple_of(ci * c, c), c), g * LANES:(g + 1) * LANES]
                    for ci, g in units]

        lw = load(lw_ref)
        cum = _each(lambda x: jnp.dot(tri, x, precision=lax.Precision.HIGHEST,
                                      preferred_element_type=F32), lw)
        total = _each(lambda x: x[c - 1:c, :], cum)
        p_in = _each(jnp.exp, cum)
        p_ex = _each(lambda x, y: jnp.exp(x - y), cum, lw)
        p_inv = _each(lambda x: jnp.exp(-x), cum)
        p_rest = _each(lambda t, x: jnp.exp(t - x), total, cum)
        r, k, v, a, b = load(r_ref), load(k_ref), load(v_ref), load(a_ref), load(b_ref)
        a_s = _each(lambda x, p: stack(x * p), a, p_ex)
        r_s = _each(lambda x, p: stack(x * p), r, p_in)
        b_s = _each(lambda x, p: stack(x * p), b, p_inv)
        k_s = _each(lambda x, p: stack(x * p), k, p_inv)
        bh_s = _each(lambda x, p: stack(x * p), b, p_rest)
        kh_s = _each(lambda x, p: stack(x * p), k, p_rest)
        v_s = _each(stack, v)

        g_ab = _each(lambda x, y: jnp.where(strict, _dot_nt(x, y), 0.0), a_s, b_s)
        g_ak = _each(lambda x, y: jnp.where(strict, _dot_nt(x, y), 0.0).astype(BF16), a_s, k_s)
        g_rb = _each(lambda x, y: jnp.where(incl, _dot_nt(x, y), 0.0).astype(BF16), r_s, b_s)
        g_rk = _each(lambda x, y: jnp.where(incl, _dot_nt(x, y), 0.0).astype(BF16), r_s, k_s)
        gv = _each(_dot, g_ak, v_s)
        kv = _each(_dot_tn, v_s, kh_s)
        yv = _each(_dot, g_rk, v_s)
        t = _each(lambda x: x.astype(BF16), _unit_lower_inverse(g_ab))
        a_bar = _each(_dot, t, a_s)
        v_bar = _each(_dot, t, gv)
        for i, (ci, g) in enumerate(units):
            abar_ref[ci, g] = a_bar[i].astype(BF16)
            vbar_ref[ci, g] = v_bar[i]
            vbart_ref[ci, g] = v_bar[i].T
            kv_ref[ci, g] = kv[i]
            yv_ref[ci, g] = yv[i]
            grb_ref[ci, g] = g_rb[i]
            rs_ref[ci, g] = r_s[i]
            bhs_ref[ci, g] = bh_s[i]
            decay_ref[ci, g] = jnp.exp(total[i])
        return carry

    lax.fori_loop(0, n_chunks // chunks_per_trip, prepare, 0)

    def advance(ci, carry):
        rows = pl.ds(pl.multiple_of(ci * c, c), c)
        pairs = list(range(n_pairs))
        ht = _each(lambda g: ht_ref[g], pairs)
        htb = _each(lambda x: x.astype(BF16), ht)
        a_bar = _each(lambda g: abar_ref[ci, g], pairs)
        u_t = _each(lambda h, ab, g: _dot_nt(h, ab) + vbart_ref[ci, g], htb, a_bar, pairs)
        u = _each(lambda h, ab, g: _dot_nt(ab, h) + vbar_ref[ci, g], htb, a_bar, pairs)
        upd = _each(lambda x, g: _dot(x, bhs_ref[ci, g]), u_t, pairs)
        for g in pairs:
            ht_ref[g] = ht[g] * decay_ref[ci, g] + upd[g] + kv_ref[ci, g]
        y_h = _each(lambda h, g: _dot_nt(rs_ref[ci, g], h), htb, pairs)
        y_u = _each(lambda x, g: _dot(grb_ref[ci, g], x), u, pairs)
        for g in pairs:
            y_s = y_h[g] + y_u[g] + yv_ref[ci, g]
            y_ref[rows, g * LANES:(g + 1) * LANES] = y_s[:c] + y_s[c:]
        return carry

    lax.fori_loop(0, n_chunks, advance, 0)


def _rwkv_scan(r, lw, k, v, a, b):
    s, c = r.shape
    n_pairs = 4 if c % (4 * LANES) == 0 else 1
    wl = n_pairs * LANES
    tt = _tile(s, 512)
    n_chunks = tt // SCAN_CHUNK
    spec = pl.BlockSpec((tt, wl), lambda p, t: (t, p))

    def per_unit(dtype, rows=LANES):
        return pltpu.VMEM((n_chunks, n_pairs, rows, LANES), dtype)

    return pl.pallas_call(
        functools.partial(_scan_kernel, n_chunks=n_chunks, n_pairs=n_pairs,
                          chunks_per_trip=2 if n_chunks % 2 == 0 else 1),
        grid=(c // wl, s // tt),
        in_specs=[spec] * 6,
        out_specs=spec,
        out_shape=jax.ShapeDtypeStruct((s, c), F32),
        scratch_shapes=[
            pltpu.VMEM((n_pairs, LANES, LANES), F32),
            per_unit(BF16), per_unit(F32), per_unit(F32), per_unit(F32), per_unit(F32),
            per_unit(BF16), per_unit(BF16), per_unit(BF16), per_unit(F32, rows=1),
        ],
        compiler_params=_cparams(("parallel", "arbitrary")),
        name="rwkv_scan",
    )(r, lw, k, v, a, b)


def _rwkv_post_kernel(y_ref, bonus_ref, g_ref, w_ref, b_ref, o_ref):
    lane = lax.broadcasted_iota(jnp.int32, (LANES, LANES), 0) // RWKV_HEAD
    lane_t = lax.broadcasted_iota(jnp.int32, (LANES, LANES), 1) // RWKV_HEAD
    ones_bd = (lane == lane_t).astype(BF16)
    y = y_ref[...]
    mu = _head_sum(y, ones_bd) * (1.0 / RWKV_HEAD)
    yc = y - mu
    var = _head_sum(yc * yc, ones_bd) * (1.0 / RWKV_HEAD)
    yn = yc * lax.rsqrt(var + GN_EPS) * w_ref[...] + b_ref[...]
    o_ref[...] = ((yn + bonus_ref[...]) * g_ref[...]).astype(o_ref.dtype)


def _rwkv_post(y, bonus, g, lnx_w, lnx_b):
    s, c = y.shape
    tm = _tile(s, 256)
    tc = _tile(c, 512)
    tile = pl.BlockSpec((tm, tc), lambda i, j: (i, j))
    vec = pl.BlockSpec((1, tc), lambda i, j: (0, j))
    return pl.pallas_call(
        _rwkv_post_kernel,
        grid=(s // tm, c // tc),
        in_specs=[tile, tile, tile, vec, vec],
        out_specs=tile,
        out_shape=jax.ShapeDtypeStruct((s, c), BF16),
        compiler_params=_cparams(("parallel", "arbitrary")),
        name="rwkv_post",
    )(y, bonus, g, lnx_w, lnx_b)


def _rope_table_kernel(pos_ref, inv_ref, sign_ref, cos_out, sin_out):
    ang = pos_ref[...] * inv_ref[...]
    cos_out[...] = jnp.cos(ang)
    sin_out[...] = jnp.sin(ang) * sign_ref[...]


def _rope_tables(pos_col, inv, sign):
    s = pos_col.shape[0]
    tm = _tile(s, 512)
    vec = pl.BlockSpec((1, LANES), lambda i: (0, 0))
    tab = pl.BlockSpec((tm, LANES), lambda i: (i, 0))
    return pl.pallas_call(
        _rope_table_kernel,
        grid=(s // tm,),
        in_specs=[pl.BlockSpec((tm, 1), lambda i: (i, 0)), vec, vec],
        out_specs=[tab, tab],
        out_shape=[jax.ShapeDtypeStruct((s, LANES), F32)] * 2,
        compiler_params=_cparams(("parallel",)),
        name="rope_tables",
    )(pos_col, inv, sign)


def _rope(t, cos, sin):
    return t * cos + pltpu.roll(t, ATTN_HEAD // 2, axis=1) * sin


def _kv_prep_kernel(pk_ref, pv_ref, cos_ref, sin_ref, k_out, vt_out, kmean_out):
    cos = cos_ref[...]
    sin = sin_ref[...]
    pk = pk_ref[...]
    parts = [_rope(pk[:, j * LANES:(j + 1) * LANES], cos, sin) for j in range(pk.shape[1] // LANES)]
    kr = parts[0] if len(parts) == 1 else jnp.concatenate(parts, axis=1)
    k_out[...] = kr.astype(BF16)
    kmean_out[0] = jnp.mean(kr, axis=0, keepdims=True)
    pv = pv_ref[...]
    ones_rows = (lax.broadcasted_iota(jnp.int32, (VT_ROWS - ATTN_HEAD, MOBA_BLOCK), 0) == 0)
    for j in range(pv.shape[1] // LANES):
        vt_out[j, :ATTN_HEAD, :] = pv[:, j * LANES:(j + 1) * LANES].T.astype(BF16)
        vt_out[j, ATTN_HEAD:, :] = ones_rows.astype(BF16)


def _kv_prep(p, cos, sin, k_off, v_off, width):
    s = p.shape[0]
    tw = 2 * LANES
    nb = s // MOBA_BLOCK
    return pl.pallas_call(
        _kv_prep_kernel,
        grid=(nb, width // tw),
        in_specs=[
            pl.BlockSpec((MOBA_BLOCK, tw), lambda i, j: (i, j + k_off)),
            pl.BlockSpec((MOBA_BLOCK, tw), lambda i, j: (i, j + v_off)),
            pl.BlockSpec((MOBA_BLOCK, LANES), lambda i, j: (i, 0)),
            pl.BlockSpec((MOBA_BLOCK, LANES), lambda i, j: (i, 0)),
        ],
        out_specs=[
            pl.BlockSpec((MOBA_BLOCK, tw), lambda i, j: (i, j)),
            pl.BlockSpec((tw // LANES, VT_ROWS, MOBA_BLOCK), lambda i, j: (j, 0, i)),
            pl.BlockSpec((1, 1, tw), lambda i, j: (i, 0, j)),
        ],
        out_shape=[
            jax.ShapeDtypeStruct((s, width), BF16),
            jax.ShapeDtypeStruct((width // ATTN_HEAD, VT_ROWS, s), BF16),
            jax.ShapeDtypeStruct((nb, 1, width), F32),
        ],
        compiler_params=_cparams(("parallel", "arbitrary")),
        name="moba_kv_prep",
    )(p, p, cos, sin)


def _split_bf16(x):
    hi = x.astype(BF16)
    return hi, (x - hi.astype(F32)).astype(BF16)


def _dot_nt_3pass(a, b):
    a_hi, a_lo = _split_bf16(a)
    b_hi, b_lo = _split_bf16(b)
    return _dot_nt(a_hi, b_hi) + _dot_nt(a_hi, b_lo) + _dot_nt(a_lo, b_hi)


def _q_prep_kernel(pq_ref, cos_ref, sin_ref, kmean_ref, q_out, *, nb):
    qb = pl.program_id(0)
    cos = cos_ref[...]
    sin = sin_ref[...]
    blk = lax.broadcasted_iota(jnp.int32, (LANES, MOBA_BLOCK), 0)
    for j in range(pq_ref.shape[1] // LANES):
        q = _rope(pq_ref[:, j * LANES:(j + 1) * LANES], cos, sin)
        km = kmean_ref[:, 0, j * LANES:(j + 1) * LANES]
        if nb < LANES:
            km = jnp.concatenate([km, jnp.zeros((LANES - nb, LANES), F32)], axis=0)
        gate = jnp.where(blk < qb, _dot_nt_3pass(km, q), NEG)
        chosen = blk == qb
        for r in range(MOBA_TOPK):
            top = jnp.max(gate, axis=0, keepdims=True)
            first = jnp.min(jnp.where(gate == top, blk, LANES), axis=0, keepdims=True)
            pick = blk == first
            chosen = chosen | (pick & (r < qb))
            gate = jnp.where(pick, NEG, gate)
        bias = jnp.where(chosen, 0.0, NEG).T
        q_out[j] = jnp.concatenate([q * SCORE_SCALE, bias], axis=1).astype(BF16)


def _q_prep(p, cos, sin, kmean, q_off, heads):
    s = p.shape[0]
    nb = s // MOBA_BLOCK
    hg = 2
    assert nb <= LANES and heads % hg == 0 and q_off % hg == 0
    tab = pl.BlockSpec((MOBA_BLOCK, LANES), lambda i, h: (i, 0))
    return pl.pallas_call(
        functools.partial(_q_prep_kernel, nb=nb),
        grid=(nb, heads // hg),
        in_specs=[
            pl.BlockSpec((MOBA_BLOCK, hg * LANES), lambda i, h: (i, h + q_off // hg)),
            tab, tab,
            pl.BlockSpec((nb, 1, hg * LANES), lambda i, h: (0, 0, h)),
        ],
        out_specs=pl.BlockSpec((hg, MOBA_BLOCK, 2 * LANES), lambda i, h: (h, i, 0)),
        out_shape=jax.ShapeDtypeStruct((heads, s, 2 * LANES), BF16),
        compiler_params=_cparams(("parallel", "arbitrary")),
        name="moba_q_prep",
    )(p, cos, sin, kmean)


def _attn_kernel(q_ref, k_ref, hot_ref, vt_ref, o_ref,
                 s0_ref, s1_ref, mx0_ref, mx1_ref, m_ref, acc_ref, *, kt):
    qb = pl.program_id(1)
    q = q_ref[0]
    blocks_per_tile = kt // MOBA_BLOCK

    def scores(g, s_ref, mx_ref):
        rows = pl.ds(pl.multiple_of(g * kt, kt), kt)
        k_aug = jnp.concatenate([k_ref[rows, :], hot_ref[rows, :]], axis=1)
        s = lax.dot_general(k_aug, q, (((1,), (1,)), ((), ())), preferred_element_type=F32)
        s_ref[...] = s
        mx_ref[...] = jnp.max(s, axis=0, keepdims=True)

    def softmax_pv(g, s, s_max):
        m = m_ref[...]
        m_new = jnp.maximum(m, s_max)
        alpha = jnp.exp2(m - m_new)
        p = jnp.exp2((s - m_new).astype(BF16))
        vt = vt_ref[0, :, pl.ds(pl.multiple_of(g * kt, kt), kt)]
        acc_ref[...] = alpha * acc_ref[...] + jnp.dot(vt, p, preferred_element_type=F32)
        m_ref[...] = m_new

    n_past = qb // blocks_per_tile
    m_ref[...] = jnp.full(m_ref.shape, NEG, F32)
    acc_ref[...] = jnp.zeros(acc_ref.shape, F32)
    scores(0, s0_ref, mx0_ref)

    def pair(j, carry):
        g = 2 * j
        scores(g + 1, s1_ref, mx1_ref)
        softmax_pv(g, s0_ref[...], mx0_ref[...])
        scores(g + 2, s0_ref, mx0_ref)
        softmax_pv(g + 1, s1_ref[...], mx1_ref[...])
        return carry

    lax.fori_loop(0, n_past // 2, pair, 0)

    own = pl.ds(pl.multiple_of((qb % blocks_per_tile) * MOBA_BLOCK, MOBA_BLOCK), MOBA_BLOCK)
    causal = (lax.broadcasted_iota(jnp.int32, (MOBA_BLOCK, MOBA_BLOCK), 0)
              <= lax.broadcasted_iota(jnp.int32, (MOBA_BLOCK, MOBA_BLOCK), 1))

    def last_tile(s_ref):
        s_ref[own, :] = jnp.where(causal, s_ref[own, :], NEG)
        s = s_ref[...]
        softmax_pv(n_past, s, jnp.max(s, axis=0, keepdims=True))

    @pl.when(n_past % 2 == 1)
    def _():
        scores(n_past, s1_ref, mx1_ref)
        softmax_pv(n_past - 1, s0_ref[...], mx0_ref[...])
        last_tile(s1_ref)

    @pl.when(n_past % 2 == 0)
    def _():
        last_tile(s0_ref)

    acc = acc_ref[...]
    out_t = acc[:ATTN_HEAD] / acc[ATTN_HEAD:ATTN_HEAD + 1]
    o_ref[...] = out_t.T.astype(o_ref.dtype)


def _moba_attention(q_aug, k, vt):
    heads, s, _ = q_aug.shape
    kt = _tile(s, ATTN_KEY_TILE)
    block_onehot = (jnp.arange(s, dtype=jnp.int32)[:, None] // MOBA_BLOCK
                    == jnp.arange(LANES, dtype=jnp.int32)[None, :]).astype(BF16)
    return pl.pallas_call(
        functools.partial(_attn_kernel, kt=kt),
        grid=(heads, s // MOBA_BLOCK),
        in_specs=[
            pl.BlockSpec((1, MOBA_BLOCK, 2 * LANES), lambda h, i: (h, i, 0)),
            pl.BlockSpec((s, ATTN_HEAD), lambda h, i: (0, h)),
            pl.BlockSpec((s, LANES), lambda h, i: (0, 0)),
            pl.BlockSpec((1, VT_ROWS, s), lambda h, i: (h, 0, 0)),
        ],
        out_specs=pl.BlockSpec((MOBA_BLOCK, ATTN_HEAD), lambda h, i: (i, h)),
        out_shape=jax.ShapeDtypeStruct((s, heads * ATTN_HEAD), BF16),
        scratch_shapes=[
            pltpu.VMEM((kt, MOBA_BLOCK), F32), pltpu.VMEM((kt, MOBA_BLOCK), F32),
            pltpu.VMEM((1, MOBA_BLOCK), F32), pltpu.VMEM((1, MOBA_BLOCK), F32),
            pltpu.VMEM((1, MOBA_BLOCK), F32), pltpu.VMEM((VT_ROWS, MOBA_BLOCK), F32),
        ],
        compiler_params=_cparams(("parallel", "arbitrary")),
        name="moba_attention",
    )(q_aug, k, block_onehot, vt)


def _ffn_block(x, xb, w_gate, w_up, w_down, g, b, alpha, name):
    f = w_gate.shape[1]
    fp = _pad_to(f, 256)
    wg = jnp.pad(w_gate, ((0, 0), (0, fp - f))).astype(BF16)
    wu = jnp.pad(w_up, ((0, 0), (0, fp - f))).astype(BF16)
    wd = jnp.pad(w_down, ((0, fp - f), (0, 0))).astype(BF16)
    (h,) = _matmul([xb], [wg, wu], [(0, 0), (0, 1)], _swiglu_epilogue, [BF16],
                   tm=1024, tn=256, name=name + "_up")
    (y,) = _matmul([h], [wd], [(0, 0)], _identity_epilogue, [F32],
                   tm=512, tn=256, name=name + "_down")
    return _residual_ln(x, y, g, b, alpha, 0.5, name + "_ln")


def _token_mixing(x1b, positions, w_in, shift_mix, decay_w0, decay_up, iclr_a0, iclr_up, gate_up,
                  k_k, k_a, r_k, lnx_w, lnx_b, w_o_rwkv, w_o_attn, w_out):
    s, d = x1b.shape
    c = decay_w0.shape[0]
    gate_rank = gate_up.shape[0]
    decay_rank, iclr_rank = decay_up.shape[0], iclr_up.shape[0]
    rwkv_in = 3 * c + decay_rank + iclr_rank + gate_rank
    cw = w_o_attn.shape[0]
    heads = cw // ATTN_HEAD

    lora_w = _pad_to(decay_rank + iclr_rank + gate_rank, 256)
    pad = lora_w - (decay_rank + iclr_rank + gate_rank)
    w_in_b = w_in.astype(BF16)
    w_in_p = jnp.concatenate(
        [w_in_b[:, :rwkv_in], jnp.zeros((d, pad), BF16), w_in_b[:, rwkv_in:]], axis=1)
    mix_p = jnp.pad(shift_mix, (0, pad)).reshape(1, -1)
    gate_up_p = jnp.pad(gate_up, ((0, pad), (0, 0))).astype(BF16)
    rwkv_w = 3 * c + lora_w

    (p,) = _matmul([x1b], [w_in_p], [(0, 0)], _identity_epilogue, [F32],
                   tm=2048, tn=256, name="in_proj")

    r, lw, k_mod, v, a, b, g, bonus = _rwkv_prep(
        p, c, lora_w, mix_p, decay_w0.reshape(1, c), iclr_a0.reshape(1, c), k_k.reshape(1, c),
        k_a.reshape(1, c), r_k.reshape(1, c), decay_up.astype(BF16), iclr_up.astype(BF16), gate_up_p)
    y = _rwkv_scan(r, lw, k_mod, v, a, b)
    yg = _rwkv_post(y, bonus, g, lnx_w.reshape(1, c), lnx_b.reshape(1, c))

    half = ATTN_HEAD // 2
    inv_half = ROPE_THETA ** (-jnp.arange(0, ATTN_HEAD, 2, dtype=F32) / ATTN_HEAD)
    inv = jnp.concatenate([inv_half, inv_half]).reshape(1, ATTN_HEAD)
    sign = jnp.concatenate([-jnp.ones((half,), F32), jnp.ones((half,), F32)]).reshape(1, ATTN_HEAD)
    pos_col = positions.astype(F32).reshape(s, 1)
    assert rwkv_w % (2 * LANES) == 0 and cw % (2 * LANES) == 0
    cos, sin = _rope_tables(pos_col, inv, sign)
    k_rope, v_b, kmean = _kv_prep(p, cos, sin, (rwkv_w + cw) // (2 * LANES),
                                  (rwkv_w + 2 * cw) // (2 * LANES), cw)
    q_aug = _q_prep(p, cos, sin, kmean, rwkv_w // LANES, heads)
    attn = _moba_attention(q_aug, k_rope, v_b)

    gate_off = (rwkv_w + 3 * cw) // 256
    (merged,) = _matmul([yg, attn], [w_o_rwkv.astype(BF16), w_o_attn.astype(BF16)], [(0, 0), (1, 1)],
                        _gated_merge_epilogue, [BF16], tm=1024, tn=256,
                        extras=[(p, gate_off), (p, gate_off + d // 256)], name="branch_out")
    (mix,) = _matmul([merged], [w_out.astype(BF16)], [(0, 0)], _identity_epilogue, [F32],
                     tm=2048, tn=256, name="mix_out")
    return mix


def kernel(x, positions, ffn1_w_gate, ffn1_w_up, ffn1_w_down, ln1_g, ln1_b, w_in, shift_mix, decay_w0, decay_up, iclr_a0, iclr_up, gate_up, k_k, k_a, r_k, lnx_w, lnx_b, w_o_rwkv, w_o_attn, w_out, ln2_g, ln2_b, ffn2_w_gate, ffn2_w_up, ffn2_w_down, ln3_g, ln3_b):
    bsz, s, d = x.shape
    depth = w_in.shape[0]
    alpha = (2 * depth) ** 0.25
    outs = []
    for bi in range(bsz):
        xf = x[bi]
        xb = xf.astype(BF16)
        for l in range(depth):
            xf, xb = _ffn_block(xf, xb, ffn1_w_gate[l], ffn1_w_up[l], ffn1_w_down[l],
                                ln1_g[l], ln1_b[l], alpha, "ffn1")
            mix = _token_mixing(xb, positions[bi], w_in[l], shift_mix[l], decay_w0[l], decay_up[l],
                                iclr_a0[l], iclr_up[l], gate_up[l], k_k[l], k_a[l], r_k[l],
                                lnx_w[l], lnx_b[l], w_o_rwkv[l], w_o_attn[l], w_out[l])
            xf, xb = _residual_ln(xf, mix, ln2_g[l], ln2_b[l], alpha, 1.0, "ln2")
            xf, xb = _ffn_block(xf, xb, ffn2_w_gate[l], ffn2_w_up[l], ffn2_w_down[l],
                                ln3_g[l], ln3_b[l], alpha, "ffn2")
        outs.append(xf)
    return jnp.stack(outs)
```

```python
import functools

import jax
import jax.numpy as jnp
from jax import lax
from jax.experimental import pallas as pl
from jax.experimental.pallas import tpu as pltpu

F32 = jnp.float32
BF16 = jnp.bfloat16

LN_EPS = 1e-5
GN_EPS = 64e-5
RWKV_HEAD = 64
ATTN_HEAD = 128
MOBA_BLOCK = 256
MOBA_TOPK = 3
ROPE_THETA = 10000.0
NEG = -1e30

SCORE_SCALE = ATTN_HEAD ** -0.5 * 1.4426950408889634
ATTN_KEY_TILE = 1024
ATTN_Q_BLOCKS = 4

VT_ROWS = ATTN_HEAD + 16

LANES = 128
SCAN_CHUNK = 64
VMEM_LIMIT = 56 * 1024 * 1024


def _cparams(sem):
    return pltpu.CompilerParams(dimension_semantics=sem, vmem_limit_bytes=VMEM_LIMIT)


def _pad_to(n, m):
    return (n + m - 1) // m * m


def _tile(n, pref):
    t = min(n, pref)
    assert n % t == 0, (n, t)
    return t


def _mm_kernel(*refs, na, nb, ne, dots, epilogue):
    a_refs = refs[:na]
    b_refs = refs[na:na + nb]
    e_refs = refs[na + nb:na + nb + ne]
    o_refs = refs[na + nb + ne:]
    accs = [jnp.dot(a_refs[i][...], b_refs[j][...], preferred_element_type=F32) for i, j in dots]
    outs = epilogue(accs, [e[...] for e in e_refs])
    for o, val in zip(o_refs, outs):
        o[...] = val.astype(o.dtype)


def _matmul(a_list, b_list, dots, epilogue, out_dtypes, *, tm, tn, extras=(), name):
    m = a_list[0].shape[0]
    n = b_list[0].shape[1]
    tm = _tile(m, tm)
    tn = _tile(n, tn)
    in_specs = [pl.BlockSpec((tm, a.shape[1]), lambda i, j: (i, 0)) for a in a_list]
    in_specs += [pl.BlockSpec((b.shape[0], tn), lambda i, j: (0, j)) for b in b_list]
    for arr, off in extras:
        if arr.shape[0] == 1:
            in_specs.append(pl.BlockSpec((1, tn), lambda i, j, off=off: (0, j + off)))
        else:
            in_specs.append(pl.BlockSpec((tm, tn), lambda i, j, off=off: (i, j + off)))
    kern = functools.partial(_mm_kernel, na=len(a_list), nb=len(b_list), ne=len(extras),
                             dots=dots, epilogue=epilogue)
    outs = pl.pallas_call(
        kern,
        grid=(m // tm, n // tn),
        in_specs=in_specs,
        out_specs=[pl.BlockSpec((tm, tn), lambda i, j: (i, j)) for _ in out_dtypes],
        out_shape=[jax.ShapeDtypeStruct((m, n), dt) for dt in out_dtypes],
        compiler_params=_cparams(("parallel", "arbitrary")),
        name=name,
    )(*a_list, *b_list, *[e for e, _ in extras])
    return outs


def _sigmoid(x):
    return 1.0 / (1.0 + jnp.exp(-x))


def _swiglu_epilogue(accs, _):
    g, u = accs
    return [g * _sigmoid(g) * u]


def _identity_epilogue(accs, _):
    return [accs[0]]


def _gated_merge_epilogue(accs, extras):
    yr, ya = accs
    pr, pa = extras
    return [_sigmoid(pr) * yr + _sigmoid(pa) * ya]


def _ln_kernel(x_ref, y_ref, g_ref, b_ref, o_ref, ob_ref, *, alpha, beta):
    z = alpha * x_ref[...] + beta * y_ref[...]
    mu = jnp.mean(z, axis=-1, keepdims=True)
    zc = z - mu
    var = jnp.mean(zc * zc, axis=-1, keepdims=True)
    out = zc * lax.rsqrt(var + LN_EPS) * g_ref[...] + b_ref[...]
    o_ref[...] = out
    ob_ref[...] = out.astype(BF16)


def _residual_ln(x, y, g, b, alpha, beta, name):
    s, d = x.shape
    tm = _tile(s, 256)
    row = pl.BlockSpec((tm, d), lambda i: (i, 0))
    vec = pl.BlockSpec((1, d), lambda i: (0, 0))
    return pl.pallas_call(
        functools.partial(_ln_kernel, alpha=alpha, beta=beta),
        grid=(s // tm,),
        in_specs=[row, row, vec, vec],
        out_specs=[row, row],
        out_shape=[jax.ShapeDtypeStruct((s, d), F32), jax.ShapeDtypeStruct((s, d), BF16)],
        compiler_params=_cparams(("parallel",)),
        name=name,
    )(x, y, g.reshape(1, d), b.reshape(1, d))


def _head_sum(x, ones_bd):
    hi = x.astype(BF16)
    rest = x - hi.astype(F32)
    mid = rest.astype(BF16)
    lo = (rest - mid.astype(F32)).astype(BF16)
    parts = []
    for j in range(x.shape[1] // LANES):
        lanes = slice(j * LANES, (j + 1) * LANES)
        parts.append(jnp.dot(hi[:, lanes], ones_bd, preferred_element_type=F32)
                     + jnp.dot(mid[:, lanes], ones_bd, preferred_element_type=F32)
                     + jnp.dot(lo[:, lanes], ones_bd, preferred_element_type=F32))
    return parts[0] if len(parts) == 1 else jnp.concatenate(parts, axis=1)


def _shifted(cur, prev8, mix, first):
    rolled = pltpu.roll(cur, 1, axis=0)
    last = jnp.where(first, 0.0, prev8[7:8, :])
    row0 = lax.broadcasted_iota(jnp.int32, cur.shape, 0) == 0
    prev = jnp.where(row0, last, rolled)
    return cur + (prev - cur) * mix


def _rwkv_prep_kernel(pr_ref, pk_ref, pv_ref, pl_ref, pr8_ref, pk8_ref, pv8_ref, pl8_ref,
                      mr_ref, mk_ref, mv_ref, ml_ref,
                      w0_ref, a0_ref, kk_ref, ka_ref, rk_ref,
                      dup_ref, iup_ref, gup_ref,
                      r_out, lw_out, k_out, v_out, a_out, b_out, g_out, bonus_out,
                      *, decay_rank, iclr_rank):
    first = pl.program_id(0) == 0
    r = _shifted(pr_ref[...], pr8_ref[...], mr_ref[...], first)
    k = _shifted(pk_ref[...], pk8_ref[...], mk_ref[...], first)
    v = _shifted(pv_ref[...], pv8_ref[...], mv_ref[...], first)
    zl = _shifted(pl_ref[...], pl8_ref[...], ml_ref[...], first)
    dw = zl[:, :decay_rank]
    da = zl[:, decay_rank:decay_rank + iclr_rank]
    dg = zl[:, decay_rank + iclr_rank:]

    lw = jnp.dot(jnp.tanh(dw).astype(BF16), dup_ref[...], preferred_element_type=F32)
    la = jnp.dot(da.astype(BF16), iup_ref[...], preferred_element_type=F32)
    g = jnp.dot(_sigmoid(dg).astype(BF16), gup_ref[...], preferred_element_type=F32)

    u = -(w0_ref[...] + lw)
    softplus = jnp.maximum(u, 0.0) + jnp.log(1.0 + jnp.exp(-jnp.abs(u)))
    w_log = -softplus - 0.5
    log_decay = -jnp.exp(w_log)
    a_sig = _sigmoid(a0_ref[...] + la)

    lane = lax.broadcasted_iota(jnp.int32, (LANES, LANES), 0) // RWKV_HEAD
    lane_t = lax.broadcasted_iota(jnp.int32, (LANES, LANES), 1) // RWKV_HEAD
    ones_bd = (lane == lane_t).astype(BF16)

    kk = k * kk_ref[...]
    norm = jnp.sqrt(_head_sum(kk * kk, ones_bd))
    kk = kk / jnp.maximum(norm, 1e-12)
    k_mod = k * (1.0 + (a_sig - 1.0) * ka_ref[...])
    bonus = _head_sum(r * k_mod * rk_ref[...], ones_bd) * v

    r_out[...] = r.astype(r_out.dtype)
    lw_out[...] = log_decay
    k_out[...] = k_mod.astype(k_out.dtype)
    v_out[...] = v.astype(v_out.dtype)
    a_out[...] = (-kk).astype(a_out.dtype)
    b_out[...] = (kk * a_sig).astype(b_out.dtype)
    g_out[...] = g.astype(g_out.dtype)
    bonus_out[...] = bonus.astype(bonus_out.dtype)


def _rwkv_prep(p, c, lora_w, shift_mix_p, decay_w0, iclr_a0, k_k, k_a, r_k_flat,
               decay_up, iclr_up, gate_up_p):
    s = p.shape[0]
    tm = _tile(s, 256)
    tc = _tile(c, 512)
    ncb = c // tc
    assert (3 * c) % lora_w == 0 and tm % 8 == 0
    lora_blk = 3 * c // lora_w
    decay_rank, iclr_rank = decay_up.shape[0], iclr_up.shape[0]

    def col(off):
        return pl.BlockSpec((tm, tc), lambda i, j: (i, j + off))

    def col8(off):
        return pl.BlockSpec((8, tc), lambda i, j: (jnp.maximum(i * (tm // 8) - 1, 0), j + off))

    def vec(off):
        return pl.BlockSpec((1, tc), lambda i, j: (0, j + off))

    in_specs = [
        col(0), col(ncb), col(2 * ncb),
        pl.BlockSpec((tm, lora_w), lambda i, j: (i, lora_blk)),
        col8(0), col8(ncb), col8(2 * ncb),
        pl.BlockSpec((8, lora_w), lambda i, j: (jnp.maximum(i * (tm // 8) - 1, 0), lora_blk)),
        vec(0), vec(ncb), vec(2 * ncb),
        pl.BlockSpec((1, lora_w), lambda i, j: (0, lora_blk)),
        vec(0), vec(0), vec(0), vec(0), vec(0),
        pl.BlockSpec((decay_rank, tc), lambda i, j: (0, j)),
        pl.BlockSpec((iclr_rank, tc), lambda i, j: (0, j)),
        pl.BlockSpec((gate_up_p.shape[0], tc), lambda i, j: (0, j)),
    ]
    out_spec = pl.BlockSpec((tm, tc), lambda i, j: (i, j))
    outs = pl.pallas_call(
        functools.partial(_rwkv_prep_kernel, decay_rank=decay_rank, iclr_rank=iclr_rank),
        grid=(s // tm, ncb),
        in_specs=in_specs,
        out_specs=[out_spec] * 8,
        out_shape=[jax.ShapeDtypeStruct((s, c), F32 if i == 1 else BF16) for i in range(8)],
        compiler_params=_cparams(("arbitrary", "arbitrary")),
        name="rwkv_prep",
    )(p, p, p, p, p, p, p, p,
      shift_mix_p, shift_mix_p, shift_mix_p, shift_mix_p,
      decay_w0, iclr_a0, k_k, k_a, r_k_flat,
      decay_up, iclr_up, gate_up_p)
    return outs


def _dot(a, b):
    return jnp.dot(a.astype(BF16), b.astype(BF16), preferred_element_type=F32)


def _dot_nt(a, b):
    return lax.dot_general(a.astype(BF16), b.astype(BF16), (((1,), (1,)), ((), ())),
                           preferred_element_type=F32)


def _dot_tn(a, b):
    return lax.dot_general(a.astype(BF16), b.astype(BF16), (((0,), (0,)), ((), ())),
                           preferred_element_type=F32)


def _each(fn, *lists):
    return [fn(*args) for args in zip(*lists)]


def _unit_lower_inverse(a_list):
    n = a_list[0].shape[0]
    row = lax.broadcasted_iota(jnp.int32, (n, n), 0)
    col = lax.broadcasted_iota(jnp.int32, (n, n), 1)
    eye = (row == col).astype(F32)
    same16 = (row // 16) == (col // 16)
    same32 = (row // 32) == (col // 32)
    same64 = (row // 64) == (col // 64)
    off32 = same32 & jnp.logical_not(same16)
    off64 = same64 & jnp.logical_not(same32)
    a_d = _each(lambda a: jnp.where(same16, a, 0.0).astype(BF16), a_list)
    a2 = _each(_dot, a_d, a_d)
    a4 = _each(_dot, a2, a2)
    a8 = _each(_dot, a4, a4)
    a3 = _each(_dot, a_d, a2)
    a12 = _each(_dot, a4, a8)
    lo = _each(lambda x1, x2, x3: eye + x1 + x2 + x3, a_d, a2, a3)
    hi = _each(lambda x4, x8, x12: eye + x4 + x8 + x12, a4, a8, a12)
    t = _each(_dot, lo, hi)
    for mask in (off32, off64):
        a_o = _each(lambda a: jnp.where(mask, a, 0.0).astype(BF16), a_list)
        tb = _each(lambda x: x.astype(BF16), t)
        ta = _each(_dot, tb, a_o)
        tat = _each(_dot, ta, tb)
        t = _each(lambda x, y: x + y, t, tat)
    return t


def _scan_kernel(r_ref, lw_ref, k_ref, v_ref, a_ref, b_ref, y_ref, ht_ref,
                 abar_ref, vbar_ref, vbart_ref, kv_ref, yv_ref, grb_ref, rs_ref, bhs_ref, decay_ref,
                 *, n_chunks, n_pairs, chunks_per_trip):
    c = SCAN_CHUNK

    @pl.when(pl.program_id(1) == 0)
    def _():
        ht_ref[...] = jnp.zeros_like(ht_ref)

    row = lax.broadcasted_iota(jnp.int32, (2 * c, 2 * c), 0)
    col = lax.broadcasted_iota(jnp.int32, (2 * c, 2 * c), 1)
    strict = col < row
    incl = col <= row
    tri = (lax.broadcasted_iota(jnp.int32, (c, c), 1)
           <= lax.broadcasted_iota(jnp.int32, (c, c), 0)).astype(F32)
    head0 = lax.broadcasted_iota(jnp.int32, (c, LANES), 1) < RWKV_HEAD

    def stack(x):
        return jnp.concatenate([jnp.where(head0, x, 0.0), jnp.where(head0, 0.0, x)],
                               axis=0).astype(BF16)

    def prepare(trip, carry):
        units = [(trip * chunks_per_trip + dc, g)
                 for dc in range(chunks_per_trip) for g in range(n_pairs)]

        def load(ref):
            return [ref[pl.ds(pl.multiple_of(ci * c, c), c), g * LANES:(g + 1) * LANES]
                    for ci, g in units]

        lw = load(lw_ref)
        cum = _each(lambda x: jnp.dot(tri, x, precision=lax.Precision.HIGHEST,
                                      preferred_element_type=F32), lw)
        total = _each(lambda x: x[c - 1:c, :], cum)
        p_in = _each(jnp.exp, cum)
        p_ex = _each(lambda x, y: jnp.exp(x - y), cum, lw)
        p_inv = _each(lambda x: jnp.exp(-x), cum)
        p_rest = _each(lambda t, x: jnp.exp(t - x), total, cum)
        r, k, v, a, b = load(r_ref), load(k_ref), load(v_ref), load(a_ref), load(b_ref)
        a_s = _each(lambda x, p: stack(x * p), a, p_ex)
        r_s = _each(lambda x, p: stack(x * p), r, p_in)
        b_s = _each(lambda x, p: stack(x * p), b, p_inv)
        k_s = _each(lambda x, p: stack(x * p), k, p_inv)
        bh_s = _each(lambda x, p: stack(x * p), b, p_rest)
        kh_s = _each(lambda x, p: stack(x * p), k, p_rest)
        v_s = _each(stack, v)

        g_ab = _each(lambda x, y: jnp.where(strict, _dot_nt(x, y), 0.0), a_s, b_s)
        g_ak = _each(lambda x, y: jnp.where(strict, _dot_nt(x, y), 0.0).astype(BF16), a_s, k_s)
        g_rb = _each(lambda x, y: jnp.where(incl, _dot_nt(x, y), 0.0).astype(BF16), r_s, b_s)
        g_rk = _each(lambda x, y: jnp.where(incl, _dot_nt(x, y), 0.0).astype(BF16), r_s, k_s)
        gv = _each(_dot, g_ak, v_s)
        kv = _each(_dot_tn, v_s, kh_s)
        yv = _each(_dot, g_rk, v_s)
        t = _each(lambda x: x.astype(BF16), _unit_lower_inverse(g_ab))
        a_bar = _each(_dot, t, a_s)
        v_bar = _each(_dot, t, gv)
        for i, (ci, g) in enumerate(units):
            abar_ref[ci, g] = a_bar[i].astype(BF16)
            vbar_ref[ci, g] = v_bar[i]
            vbart_ref[ci, g] = v_bar[i].T
            kv_ref[ci, g] = kv[i]
            yv_ref[ci, g] = yv[i]
            grb_ref[ci, g] = g_rb[i]
            rs_ref[ci, g] = r_s[i]
            bhs_ref[ci, g] = bh_s[i]
            decay_ref[ci, g] = jnp.exp(total[i])
        return carry

    lax.fori_loop(0, n_chunks // chunks_per_trip, prepare, 0)

    def advance(ci, carry):
        rows = pl.ds(pl.multiple_of(ci * c, c), c)
        pairs = list(range(n_pairs))
        ht = _each(lambda g: ht_ref[g], pairs)
        htb = _each(lambda x: x.astype(BF16), ht)
        a_bar = _each(lambda g: abar_ref[ci, g], pairs)
        u_t = _each(lambda h, ab, g: _dot_nt(h, ab) + vbart_ref[ci, g], htb, a_bar, pairs)
        u = _each(lambda h, ab, g: _dot_nt(ab, h) + vbar_ref[ci, g], htb, a_bar, pairs)
        upd = _each(lambda x, g: _dot(x, bhs_ref[ci, g]), u_t, pairs)
        for g in pairs:
            ht_ref[g] = ht[g] * decay_ref[ci, g] + upd[g] + kv_ref[ci, g]
        y_h = _each(lambda h, g: _dot_nt(rs_ref[ci, g], h), htb, pairs)
        y_u = _each(lambda x, g: _dot(grb_ref[ci, g], x), u, pairs)
        for g in pairs:
            y_s = y_h[g] + y_u[g] + yv_ref[ci, g]
            y_ref[rows, g * LANES:(g + 1) * LANES] = y_s[:c] + y_s[c:]
        return carry

    lax.fori_loop(0, n_chunks, advance, 0)


def _rwkv_scan(r, lw, k, v, a, b):
    s, c = r.shape
    n_pairs = 4 if c % (4 * LANES) == 0 else 1
    wl = n_pairs * LANES
    tt = _tile(s, 512)
    n_chunks = tt // SCAN_CHUNK
    spec = pl.BlockSpec((tt, wl), lambda p, t: (t, p))

    def per_unit(dtype, rows=LANES):
        return pltpu.VMEM((n_chunks, n_pairs, rows, LANES), dtype)

    return pl.pallas_call(
        functools.partial(_scan_kernel, n_chunks=n_chunks, n_pairs=n_pairs,
                          chunks_per_trip=2 if n_chunks % 2 == 0 else 1),
        grid=(c // wl, s // tt),
        in_specs=[spec] * 6,
        out_specs=spec,
        out_shape=jax.ShapeDtypeStruct((s, c), F32),
        scratch_shapes=[
            pltpu.VMEM((n_pairs, LANES, LANES), F32),
            per_unit(BF16), per_unit(F32), per_unit(F32), per_unit(F32), per_unit(F32),
            per_unit(BF16), per_unit(BF16), per_unit(BF16), per_unit(F32, rows=1),
        ],
        compiler_params=_cparams(("parallel", "arbitrary")),
        name="rwkv_scan",
    )(r, lw, k, v, a, b)


def _rwkv_post_kernel(y_ref, bonus_ref, g_ref, w_ref, b_ref, o_ref):
    lane = lax.broadcasted_iota(jnp.int32, (LANES, LANES), 0) // RWKV_HEAD
    lane_t = lax.broadcasted_iota(jnp.int32, (LANES, LANES), 1) // RWKV_HEAD
    ones_bd = (lane == lane_t).astype(BF16)
    y = y_ref[...]
    mu = _head_sum(y, ones_bd) * (1.0 / RWKV_HEAD)
    yc = y - mu
    var = _head_sum(yc * yc, ones_bd) * (1.0 / RWKV_HEAD)
    yn = yc * lax.rsqrt(var + GN_EPS) * w_ref[...] + b_ref[...]
    o_ref[...] = ((yn + bonus_ref[...]) * g_ref[...]).astype(o_ref.dtype)


def _rwkv_post(y, bonus, g, lnx_w, lnx_b):
    s, c = y.shape
    tm = _tile(s, 256)
    tc = _tile(c, 512)
    tile = pl.BlockSpec((tm, tc), lambda i, j: (i, j))
    vec = pl.BlockSpec((1, tc), lambda i, j: (0, j))
    return pl.pallas_call(
        _rwkv_post_kernel,
        grid=(s // tm, c // tc),
        in_specs=[tile, tile, tile, vec, vec],
        out_specs=tile,
        out_shape=jax.ShapeDtypeStruct((s, c), BF16),
        compiler_params=_cparams(("parallel", "arbitrary")),
        name="rwkv_post",
    )(y, bonus, g, lnx_w, lnx_b)


def _rope_table_kernel(pos_ref, inv_ref, sign_ref, cos_out, sin_out):
    ang = pos_ref[...] * inv_ref[...]
    cos_out[...] = jnp.cos(ang)
    sin_out[...] = jnp.sin(ang) * sign_ref[...]


def _rope_tables(pos_col, inv, sign):
    s = pos_col.shape[0]
    tm = _tile(s, 512)
    vec = pl.BlockSpec((1, LANES), lambda i: (0, 0))
    tab = pl.BlockSpec((tm, LANES), lambda i: (i, 0))
    return pl.pallas_call(
        _rope_table_kernel,
        grid=(s // tm,),
        in_specs=[pl.BlockSpec((tm, 1), lambda i: (i, 0)), vec, vec],
        out_specs=[tab, tab],
        out_shape=[jax.ShapeDtypeStruct((s, LANES), F32)] * 2,
        compiler_params=_cparams(("parallel",)),
        name="rope_tables",
    )(pos_col, inv, sign)


def _rope(t, cos, sin):
    return t * cos + pltpu.roll(t, ATTN_HEAD // 2, axis=1) * sin


def _kv_prep_kernel(pk_ref, pv_ref, cos_ref, sin_ref, k_out, vt_out, kmean_out):
    cos = cos_ref[...]
    sin = sin_ref[...]
    pk = pk_ref[...]
    parts = [_rope(pk[:, j * LANES:(j + 1) * LANES], cos, sin) for j in range(pk.shape[1] // LANES)]
    kr = parts[0] if len(parts) == 1 else jnp.concatenate(parts, axis=1)
    k_out[...] = kr.astype(BF16)
    kmean_out[0] = jnp.mean(kr, axis=0, keepdims=True)
    pv = pv_ref[...]
    ones_rows = (lax.broadcasted_iota(jnp.int32, (VT_ROWS - ATTN_HEAD, MOBA_BLOCK), 0) == 0)
    for j in range(pv.shape[1] // LANES):
        vt_out[j, :ATTN_HEAD, :] = pv[:, j * LANES:(j + 1) * LANES].T.astype(BF16)
        vt_out[j, ATTN_HEAD:, :] = ones_rows.astype(BF16)


def _kv_prep(p, cos, sin, k_off, v_off, width):
    s = p.shape[0]
    tw = 2 * LANES
    nb = s // MOBA_BLOCK
    return pl.pallas_call(
        _kv_prep_kernel,
        grid=(nb, width // tw),
        in_specs=[
            pl.BlockSpec((MOBA_BLOCK, tw), lambda i, j: (i, j + k_off)),
            pl.BlockSpec((MOBA_BLOCK, tw), lambda i, j: (i, j + v_off)),
            pl.BlockSpec((MOBA_BLOCK, LANES), lambda i, j: (i, 0)),
            pl.BlockSpec((MOBA_BLOCK, LANES), lambda i, j: (i, 0)),
        ],
        out_specs=[
            pl.BlockSpec((MOBA_BLOCK, tw), lambda i, j: (i, j)),
            pl.BlockSpec((tw // LANES, VT_ROWS, MOBA_BLOCK), lambda i, j: (j, 0, i)),
            pl.BlockSpec((1, 1, tw), lambda i, j: (i, 0, j)),
        ],
        out_shape=[
            jax.ShapeDtypeStruct((s, width), BF16),
            jax.ShapeDtypeStruct((width // ATTN_HEAD, VT_ROWS, s), BF16),
            jax.ShapeDtypeStruct((nb, 1, width), F32),
        ],
        compiler_params=_cparams(("parallel", "arbitrary")),
        name="moba_kv_prep",
    )(p, p, cos, sin)


def _split_bf16(x):
    hi = x.astype(BF16)
    return hi, (x - hi.astype(F32)).astype(BF16)


def _dot_nt_3pass(a, b):
    a_hi, a_lo = _split_bf16(a)
    b_hi, b_lo = _split_bf16(b)
    return _dot_nt(a_hi, b_hi) + _dot_nt(a_hi, b_lo) + _dot_nt(a_lo, b_hi)


def _q_prep_kernel(pq_ref, cos_ref, sin_ref, kmean_ref, q_out, *, nb):
    qb = pl.program_id(0)
    cos = cos_ref[...]
    sin = sin_ref[...]
    blk = lax.broadcasted_iota(jnp.int32, (LANES, MOBA_BLOCK), 0)
    for j in range(pq_ref.shape[1] // LANES):
        q = _rope(pq_ref[:, j * LANES:(j + 1) * LANES], cos, sin)
        km = kmean_ref[:, 0, j * LANES:(j + 1) * LANES]
        if nb < LANES:
            km = jnp.concatenate([km, jnp.zeros((LANES - nb, LANES), F32)], axis=0)
        gate = jnp.where(blk < qb, _dot_nt_3pass(km, q), NEG)
        chosen = blk == qb
        for r in range(MOBA_TOPK):
            top = jnp.max(gate, axis=0, keepdims=True)
            first = jnp.min(jnp.where(gate == top, blk, LANES), axis=0, keepdims=True)
            pick = blk == first
            chosen = chosen | (pick & (r < qb))
            gate = jnp.where(pick, NEG, gate)
        bias = jnp.where(chosen, 0.0, NEG).T
        q_out[j] = jnp.concatenate([q * SCORE_SCALE, bias], axis=1).astype(BF16)


def _q_prep(p, cos, sin, kmean, q_off, heads):
    s = p.shape[0]
    nb = s // MOBA_BLOCK
    hg = 2
    assert nb <= LANES and heads % hg == 0 and q_off % hg == 0
    tab = pl.BlockSpec((MOBA_BLOCK, LANES), lambda i, h: (i, 0))
    return pl.pallas_call(
        functools.partial(_q_prep_kernel, nb=nb),
        grid=(nb, heads // hg),
        in_specs=[
            pl.BlockSpec((MOBA_BLOCK, hg * LANES), lambda i, h: (i, h + q_off // hg)),
            tab, tab,
            pl.BlockSpec((nb, 1, hg * LANES), lambda i, h: (0, 0, h)),
        ],
        out_specs=pl.BlockSpec((hg, MOBA_BLOCK, 2 * LANES), lambda i, h: (h, i, 0)),
        out_shape=jax.ShapeDtypeStruct((heads, s, 2 * LANES), BF16),
        compiler_params=_cparams(("parallel", "arbitrary")),
        name="moba_q_prep",
    )(p, cos, sin, kmean)


def _attn_kernel(q_ref, k_ref, hot_ref, vt_ref, o_ref,
                 s0_ref, s1_ref, mx0_ref, mx1_ref, m_ref, acc_ref, *, kt):
    q_blocks = q_ref.shape[1] // MOBA_BLOCK
    first_qb = pl.program_id(1) * q_blocks
    q = q_ref[0]
    blocks_per_tile = kt // MOBA_BLOCK

    def scores(g, s_ref, mx_ref):
        rows = pl.ds(pl.multiple_of(g * kt, kt), kt)
        k_aug = jnp.concatenate([k_ref[rows, :], hot_ref[rows, :]], axis=1)
        s = lax.dot_general(k_aug, q, (((1,), (1,)), ((), ())), preferred_element_type=F32)
        s_ref[...] = s
        mx_ref[...] = jnp.max(s, axis=0, keepdims=True)

    def softmax_pv(g, s, s_max):
        m = m_ref[...]
        m_new = jnp.maximum(m, s_max)
        alpha = jnp.exp2(m - m_new)
        p = jnp.exp2((s - m_new).astype(BF16))
        vt = vt_ref[0, :, pl.ds(pl.multiple_of(g * kt, kt), kt)]
        acc_ref[...] = alpha * acc_ref[...] + jnp.dot(vt, p, preferred_element_type=F32)
        m_ref[...] = m_new

    n_past = first_qb // blocks_per_tile
    m_ref[...] = jnp.full(m_ref.shape, NEG, F32)
    acc_ref[...] = jnp.zeros(acc_ref.shape, F32)
    scores(0, s0_ref, mx0_ref)

    def pair(j, carry):
        g = 2 * j
        scores(g + 1, s1_ref, mx1_ref)
        softmax_pv(g, s0_ref[...], mx0_ref[...])
        scores(g + 2, s0_ref, mx0_ref)
        softmax_pv(g + 1, s1_ref[...], mx1_ref[...])
        return carry

    lax.fori_loop(0, n_past // 2, pair, 0)

    causal = (lax.broadcasted_iota(jnp.int32, (MOBA_BLOCK, MOBA_BLOCK), 0)
              <= lax.broadcasted_iota(jnp.int32, (MOBA_BLOCK, MOBA_BLOCK), 1))

    def last_tile(s_ref):
        for c in range(q_blocks):
            own = pl.ds(pl.multiple_of(((first_qb + c) % blocks_per_tile) * MOBA_BLOCK, MOBA_BLOCK),
                        MOBA_BLOCK)
            cols = slice(c * MOBA_BLOCK, (c + 1) * MOBA_BLOCK)
            s_ref[own, cols] = jnp.where(causal, s_ref[own, cols], NEG)
        s = s_ref[...]
        softmax_pv(n_past, s, jnp.max(s, axis=0, keepdims=True))

    @pl.when(n_past % 2 == 1)
    def _():
        scores(n_past, s1_ref, mx1_ref)
        softmax_pv(n_past - 1, s0_ref[...], mx0_ref[...])
        last_tile(s1_ref)

    @pl.when(n_past % 2 == 0)
    def _():
        last_tile(s0_ref)

    acc = acc_ref[...]
    out_t = acc[:ATTN_HEAD] / acc[ATTN_HEAD:ATTN_HEAD + 1]
    o_ref[...] = out_t.T.astype(o_ref.dtype)


def _moba_attention(q_aug, k, vt):
    heads, s, _ = q_aug.shape
    kt = _tile(s, ATTN_KEY_TILE)
    block_onehot = (jnp.arange(s, dtype=jnp.int32)[:, None] // MOBA_BLOCK
                    == jnp.arange(LANES, dtype=jnp.int32)[None, :]).astype(BF16)
    q_blocks = ATTN_Q_BLOCKS if (kt // MOBA_BLOCK) % ATTN_Q_BLOCKS == 0 else 1
    qw = q_blocks * MOBA_BLOCK
    return pl.pallas_call(
        functools.partial(_attn_kernel, kt=kt),
        grid=(heads, s // qw),
        in_specs=[
            pl.BlockSpec((1, qw, 2 * LANES), lambda h, i: (h, i, 0)),
            pl.BlockSpec((s, ATTN_HEAD), lambda h, i: (0, h)),
            pl.BlockSpec((s, LANES), lambda h, i: (0, 0)),
            pl.BlockSpec((1, VT_ROWS, s), lambda h, i: (h, 0, 0)),
        ],
        out_specs=pl.BlockSpec((qw, ATTN_HEAD), lambda h, i: (i, h)),
        out_shape=jax.ShapeDtypeStruct((s, heads * ATTN_HEAD), BF16),
        scratch_shapes=[
            pltpu.VMEM((kt, qw), F32), pltpu.VMEM((kt, qw), F32),
            pltpu.VMEM((1, qw), F32), pltpu.VMEM((1, qw), F32),
            pltpu.VMEM((1, qw), F32), pltpu.VMEM((VT_ROWS, qw), F32),
        ],
        compiler_params=_cparams(("parallel", "arbitrary")),
        name="moba_attention",
    )(q_aug, k, block_onehot, vt)


def _ffn_block(x, xb, w_gate, w_up, w_down, g, b, alpha, name):
    f = w_gate.shape[1]
    fp = _pad_to(f, 256)
    wg = jnp.pad(w_gate, ((0, 0), (0, fp - f))).astype(BF16)
    wu = jnp.pad(w_up, ((0, 0), (0, fp - f))).astype(BF16)
    wd = jnp.pad(w_down, ((0, fp - f), (0, 0))).astype(BF16)
    (h,) = _matmul([xb], [wg, wu], [(0, 0), (0, 1)], _swiglu_epilogue, [BF16],
                   tm=1024, tn=256, name=name + "_up")
    (y,) = _matmul([h], [wd], [(0, 0)], _identity_epilogue, [F32],
                   tm=512, tn=256, name=name + "_down")
    return _residual_ln(x, y, g, b, alpha, 0.5, name + "_ln")


def _token_mixing(x1b, positions, w_in, shift_mix, decay_w0, decay_up, iclr_a0, iclr_up, gate_up,
                  k_k, k_a, r_k, lnx_w, lnx_b, w_o_rwkv, w_o_attn, w_out):
    s, d = x1b.shape
    c = decay_w0.shape[0]
    gate_rank = gate_up.shape[0]
    decay_rank, iclr_rank = decay_up.shape[0], iclr_up.shape[0]
    rwkv_in = 3 * c + decay_rank + iclr_rank + gate_rank
    cw = w_o_attn.shape[0]
    heads = cw // ATTN_HEAD

    lora_w = _pad_to(decay_rank + iclr_rank + gate_rank, 256)
    pad = lora_w - (decay_rank + iclr_rank + gate_rank)
    w_rwkv = jnp.pad(w_in[:, :rwkv_in].astype(BF16), ((0, 0), (0, pad)))
    w_attn = w_in[:, rwkv_in:].astype(BF16)
    mix_p = jnp.pad(shift_mix, (0, pad)).reshape(1, -1)
    gate_up_p = jnp.pad(gate_up, ((0, pad), (0, 0))).astype(BF16)

    (p_rwkv,) = _matmul([x1b], [w_rwkv], [(0, 0)], _identity_epilogue, [F32],
                        tm=2048, tn=256, name="in_proj_rwkv")
    (p_attn,) = _matmul([x1b], [w_attn], [(0, 0)], _identity_epilogue, [F32],
                        tm=2048, tn=256, name="in_proj_attn")

    r, lw, k_mod, v, a, b, g, bonus = _rwkv_prep(
        p_rwkv, c, lora_w, mix_p, decay_w0.reshape(1, c), iclr_a0.reshape(1, c), k_k.reshape(1, c),
        k_a.reshape(1, c), r_k.reshape(1, c), decay_up.astype(BF16), iclr_up.astype(BF16), gate_up_p)
    y = _rwkv_scan(r, lw, k_mod, v, a, b)
    yg = _rwkv_post(y, bonus, g, lnx_w.reshape(1, c), lnx_b.reshape(1, c))

    half = ATTN_HEAD // 2
    inv_half = ROPE_THETA ** (-jnp.arange(0, ATTN_HEAD, 2, dtype=F32) / ATTN_HEAD)
    inv = jnp.concatenate([inv_half, inv_half]).reshape(1, ATTN_HEAD)
    sign = jnp.concatenate([-jnp.ones((half,), F32), jnp.ones((half,), F32)]).reshape(1, ATTN_HEAD)
    pos_col = positions.astype(F32).reshape(s, 1)
    assert cw % (2 * LANES) == 0
    cos, sin = _rope_tables(pos_col, inv, sign)
    k_rope, v_b, kmean = _kv_prep(p_attn, cos, sin, cw // (2 * LANES), 2 * cw // (2 * LANES), cw)
    q_aug = _q_prep(p_attn, cos, sin, kmean, 0, heads)
    attn = _moba_attention(q_aug, k_rope, v_b)

    gate_off = 3 * cw // 256
    (merged,) = _matmul([yg, attn], [w_o_rwkv.astype(BF16), w_o_attn.astype(BF16)], [(0, 0), (1, 1)],
                        _gated_merge_epilogue, [BF16], tm=1024, tn=256,
                        extras=[(p_attn, gate_off), (p_attn, gate_off + d // 256)], name="branch_out")
    (mix,) = _matmul([merged], [w_out.astype(BF16)], [(0, 0)], _identity_epilogue, [F32],
                     tm=2048, tn=256, name="mix_out")
    return mix


def kernel(x, positions, ffn1_w_gate, ffn1_w_up, ffn1_w_down, ln1_g, ln1_b, w_in, shift_mix, decay_w0, decay_up, iclr_a0, iclr_up, gate_up, k_k, k_a, r_k, lnx_w, lnx_b, w_o_rwkv, w_o_attn, w_out, ln2_g, ln2_b, ffn2_w_gate, ffn2_w_up, ffn2_w_down, ln3_g, ln3_b):
    bsz, s, d = x.shape
    depth = w_in.shape[0]
    alpha = (2 * depth) ** 0.25
    outs = []
    for bi in range(bsz):
        xf = x[bi]
        xb = xf.astype(BF16)
        for l in range(depth):
            xf, xb = _ffn_block(xf, xb, ffn1_w_gate[l], ffn1_w_up[l], ffn1_w_down[l],
                                ln1_g[l], ln1_b[l], alpha, "ffn1")
            mix = _token_mixing(xb, positions[bi], w_in[l], shift_mix[l], decay_w0[l], decay_up[l],
                                iclr_a0[l], iclr_up[l], gate_up[l], k_k[l], k_a[l], r_k[l],
                                lnx_w[l], lnx_b[l], w_o_rwkv[l], w_o_attn[l], w_out[l])
            xf, xb = _residual_ln(xf, mix, ln2_g[l], ln2_b[l], alpha, 1.0, "ln2")
            xf, xb = _ffn_block(xf, xb, ffn2_w_gate[l], ffn2_w_up[l], ffn2_w_down[l],
                                ln3_g[l], ln3_b[l], alpha, "ffn2")
        outs.append(xf)
    return jnp.stack(outs)
```

```python
import functools

import jax
import jax.numpy as jnp
from jax import lax
from jax.experimental import pallas as pl
from jax.experimental.pallas import tpu as pltpu

F32 = jnp.float32
BF16 = jnp.bfloat16

LN_EPS = 1e-5
GN_EPS = 64e-5
RWKV_HEAD = 64
ATTN_HEAD = 128
MOBA_BLOCK = 256
MOBA_TOPK = 3
ROPE_THETA = 10000.0
NEG = -1e30

SCORE_SCALE = ATTN_HEAD ** -0.5 * 1.4426950408889634
ATTN_KEY_TILE = 1024
ATTN_Q_BLOCKS = 4

VT_ROWS = ATTN_HEAD + 16

LANES = 128
SCAN_CHUNK = 64
VMEM_LIMIT = 56 * 1024 * 1024


def _cparams(sem):
    return pltpu.CompilerParams(dimension_semantics=sem, vmem_limit_bytes=VMEM_LIMIT)


def _pad_to(n, m):
    return (n + m - 1) // m * m


def _tile(n, pref):
    t = min(n, pref)
    assert n % t == 0, (n, t)
    return t


def _mm_kernel(*refs, na, nb, ne, dots, epilogue):
    a_refs = refs[:na]
    b_refs = refs[na:na + nb]
    e_refs = refs[na + nb:na + nb + ne]
    o_refs = refs[na + nb + ne:]
    accs = [jnp.dot(a_refs[i][...], b_refs[j][...], preferred_element_type=F32) for i, j in dots]
    outs = epilogue(accs, [e[...] for e in e_refs])
    for o, val in zip(o_refs, outs):
        o[...] = val.astype(o.dtype)


def _matmul(a_list, b_list, dots, epilogue, out_dtypes, *, tm, tn, extras=(), name):
    m = a_list[0].shape[0]
    n = b_list[0].shape[1]
    tm = _tile(m, tm)
    tn = _tile(n, tn)
    in_specs = [pl.BlockSpec((tm, a.shape[1]), lambda i, j: (i, 0)) for a in a_list]
    in_specs += [pl.BlockSpec((b.shape[0], tn), lambda i, j: (0, j)) for b in b_list]
    for arr, off in extras:
        if arr.shape[0] == 1:
            in_specs.append(pl.BlockSpec((1, tn), lambda i, j, off=off: (0, j + off)))
        else:
            in_specs.append(pl.BlockSpec((tm, tn), lambda i, j, off=off: (i, j + off)))
    kern = functools.partial(_mm_kernel, na=len(a_list), nb=len(b_list), ne=len(extras),
                             dots=dots, epilogue=epilogue)
    outs = pl.pallas_call(
        kern,
        grid=(m // tm, n // tn),
        in_specs=in_specs,
        out_specs=[pl.BlockSpec((tm, tn), lambda i, j: (i, j)) for _ in out_dtypes],
        out_shape=[jax.ShapeDtypeStruct((m, n), dt) for dt in out_dtypes],
        compiler_params=_cparams(("parallel", "arbitrary")),
        name=name,
    )(*a_list, *b_list, *[e for e, _ in extras])
    return outs


def _sigmoid(x):
    return 0.5 * jnp.tanh(0.5 * x) + 0.5


def _swiglu_epilogue(accs, _):
    g, u = accs
    return [g * _sigmoid(g) * u]


def _identity_epilogue(accs, _):
    return [accs[0]]


def _gated_merge_epilogue(accs, extras):
    yr, ya = accs
    pr, pa = extras
    return [_sigmoid(pr) * yr + _sigmoid(pa) * ya]


def _ln_kernel(x_ref, y_ref, g_ref, b_ref, o_ref, ob_ref, *, alpha, beta):
    z = alpha * x_ref[...] + beta * y_ref[...]
    mu = jnp.mean(z, axis=-1, keepdims=True)
    zc = z - mu
    var = jnp.mean(zc * zc, axis=-1, keepdims=True)
    out = zc * lax.rsqrt(var + LN_EPS) * g_ref[...] + b_ref[...]
    o_ref[...] = out
    ob_ref[...] = out.astype(BF16)


def _residual_ln(x, y, g, b, alpha, beta, name):
    s, d = x.shape
    tm = _tile(s, 256)
    row = pl.BlockSpec((tm, d), lambda i: (i, 0))
    vec = pl.BlockSpec((1, d), lambda i: (0, 0))
    return pl.pallas_call(
        functools.partial(_ln_kernel, alpha=alpha, beta=beta),
        grid=(s // tm,),
        in_specs=[row, row, vec, vec],
        out_specs=[row, row],
        out_shape=[jax.ShapeDtypeStruct((s, d), F32), jax.ShapeDtypeStruct((s, d), BF16)],
        compiler_params=_cparams(("parallel",)),
        name=name,
    )(x, y, g.reshape(1, d), b.reshape(1, d))


def _head_sum(x, ones_bd):
    hi = x.astype(BF16)
    rest = x - hi.astype(F32)
    mid = rest.astype(BF16)
    lo = (rest - mid.astype(F32)).astype(BF16)
    parts = []
    for j in range(x.shape[1] // LANES):
        lanes = slice(j * LANES, (j + 1) * LANES)
        parts.append(jnp.dot(hi[:, lanes], ones_bd, preferred_element_type=F32)
                     + jnp.dot(mid[:, lanes], ones_bd, preferred_element_type=F32)
                     + jnp.dot(lo[:, lanes], ones_bd, preferred_element_type=F32))
    return parts[0] if len(parts) == 1 else jnp.concatenate(parts, axis=1)


def _shifted(cur, prev8, mix, first):
    rolled = pltpu.roll(cur, 1, axis=0)
    last = jnp.where(first, 0.0, prev8[7:8, :])
    row0 = lax.broadcasted_iota(jnp.int32, cur.shape, 0) == 0
    prev = jnp.where(row0, last, rolled)
    return cur + (prev - cur) * mix


def _rwkv_prep_kernel(pr_ref, pk_ref, pv_ref, pl_ref, pr8_ref, pk8_ref, pv8_ref, pl8_ref,
                      mr_ref, mk_ref, mv_ref, ml_ref,
                      w0_ref, a0_ref, kk_ref, ka_ref, rk_ref,
                      dup_ref, iup_ref, gup_ref,
                      r_out, lw_out, k_out, v_out, a_out, b_out, g_out, bonus_out, act_ref,
                      *, decay_rank, iclr_rank):
    first = pl.program_id(0) == 0
    r = _shifted(pr_ref[...], pr8_ref[...], mr_ref[...], first)
    k = _shifted(pk_ref[...], pk8_ref[...], mk_ref[...], first)
    v = _shifted(pv_ref[...], pv8_ref[...], mv_ref[...], first)

    @pl.when(pl.program_id(1) == 0)
    def _():
        zl = _shifted(pl_ref[...], pl8_ref[...], ml_ref[...], first)
        act_ref[:, :decay_rank] = jnp.tanh(zl[:, :decay_rank]).astype(BF16)
        act_ref[:, decay_rank:decay_rank + iclr_rank] = (
            zl[:, decay_rank:decay_rank + iclr_rank].astype(BF16))
        act_ref[:, decay_rank + iclr_rank:] = _sigmoid(zl[:, decay_rank + iclr_rank:]).astype(BF16)

    lw = jnp.dot(act_ref[:, :decay_rank], dup_ref[...], preferred_element_type=F32)
    la = jnp.dot(act_ref[:, decay_rank:decay_rank + iclr_rank], iup_ref[...],
                 preferred_element_type=F32)
    g = jnp.dot(act_ref[:, decay_rank + iclr_rank:], gup_ref[...], preferred_element_type=F32)

    u = -(w0_ref[...] + lw)
    softplus = jnp.maximum(u, 0.0) + jnp.log(1.0 + jnp.exp(-jnp.abs(u)))
    w_log = -softplus - 0.5
    log_decay = -jnp.exp(w_log)
    a_sig = _sigmoid(a0_ref[...] + la)

    lane = lax.broadcasted_iota(jnp.int32, (LANES, LANES), 0) // RWKV_HEAD
    lane_t = lax.broadcasted_iota(jnp.int32, (LANES, LANES), 1) // RWKV_HEAD
    ones_bd = (lane == lane_t).astype(BF16)

    kk = k * kk_ref[...]
    kk = kk * lax.rsqrt(jnp.maximum(_head_sum(kk * kk, ones_bd), 1e-24))
    k_mod = k * (1.0 + (a_sig - 1.0) * ka_ref[...])
    bonus = _head_sum(r * k_mod * rk_ref[...], ones_bd) * v

    r_out[...] = r.astype(r_out.dtype)
    lw_out[...] = log_decay
    k_out[...] = k_mod.astype(k_out.dtype)
    v_out[...] = v.astype(v_out.dtype)
    a_out[...] = (-kk).astype(a_out.dtype)
    b_out[...] = (kk * a_sig).astype(b_out.dtype)
    g_out[...] = g.astype(g_out.dtype)
    bonus_out[...] = bonus.astype(bonus_out.dtype)


def _rwkv_prep(p, c, lora_w, shift_mix_p, decay_w0, iclr_a0, k_k, k_a, r_k_flat,
               decay_up, iclr_up, gate_up_p):
    s = p.shape[0]
    tm = _tile(s, 256)
    tc = _tile(c, 512)
    ncb = c // tc
    assert (3 * c) % lora_w == 0 and tm % 8 == 0
    lora_blk = 3 * c // lora_w
    decay_rank, iclr_rank = decay_up.shape[0], iclr_up.shape[0]

    def col(off):
        return pl.BlockSpec((tm, tc), lambda i, j: (i, j + off))

    def col8(off):
        return pl.BlockSpec((8, tc), lambda i, j: (jnp.maximum(i * (tm // 8) - 1, 0), j + off))

    def vec(off):
        return pl.BlockSpec((1, tc), lambda i, j: (0, j + off))

    in_specs = [
        col(0), col(ncb), col(2 * ncb),
        pl.BlockSpec((tm, lora_w), lambda i, j: (i, lora_blk)),
        col8(0), col8(ncb), col8(2 * ncb),
        pl.BlockSpec((8, lora_w), lambda i, j: (jnp.maximum(i * (tm // 8) - 1, 0), lora_blk)),
        vec(0), vec(ncb), vec(2 * ncb),
        pl.BlockSpec((1, lora_w), lambda i, j: (0, lora_blk)),
        vec(0), vec(0), vec(0), vec(0), vec(0),
        pl.BlockSpec((decay_rank, tc), lambda i, j: (0, j)),
        pl.BlockSpec((iclr_rank, tc), lambda i, j: (0, j)),
        pl.BlockSpec((gate_up_p.shape[0], tc), lambda i, j: (0, j)),
    ]
    out_spec = pl.BlockSpec((tm, tc), lambda i, j: (i, j))
    outs = pl.pallas_call(
        functools.partial(_rwkv_prep_kernel, decay_rank=decay_rank, iclr_rank=iclr_rank),
        grid=(s // tm, ncb),
        in_specs=in_specs,
        out_specs=[out_spec] * 8,
        out_shape=[jax.ShapeDtypeStruct((s, c), F32 if i == 1 else BF16) for i in range(8)],
        scratch_shapes=[pltpu.VMEM((tm, lora_w), BF16)],
        compiler_params=_cparams(("arbitrary", "arbitrary")),
        name="rwkv_prep",
    )(p, p, p, p, p, p, p, p,
      shift_mix_p, shift_mix_p, shift_mix_p, shift_mix_p,
      decay_w0, iclr_a0, k_k, k_a, r_k_flat,
      decay_up, iclr_up, gate_up_p)
    return outs


def _dot(a, b):
    return jnp.dot(a.astype(BF16), b.astype(BF16), preferred_element_type=F32)


def _dot_nt(a, b):
    return lax.dot_general(a.astype(BF16), b.astype(BF16), (((1,), (1,)), ((), ())),
                           preferred_element_type=F32)


def _dot_tn(a, b):
    return lax.dot_general(a.astype(BF16), b.astype(BF16), (((0,), (0,)), ((), ())),
                           preferred_element_type=F32)


def _each(fn, *lists):
    return [fn(*args) for args in zip(*lists)]


def _unit_lower_inverse(a_list):
    n = a_list[0].shape[0]
    row = lax.broadcasted_iota(jnp.int32, (n, n), 0)
    col = lax.broadcasted_iota(jnp.int32, (n, n), 1)
    eye = (row == col).astype(F32)
    same16 = (row // 16) == (col // 16)
    same32 = (row // 32) == (col // 32)
    same64 = (row // 64) == (col // 64)
    off32 = same32 & jnp.logical_not(same16)
    off64 = same64 & jnp.logical_not(same32)
    a_d = _each(lambda a: jnp.where(same16, a, 0.0).astype(BF16), a_list)
    a2 = _each(_dot, a_d, a_d)
    a4 = _each(_dot, a2, a2)
    a8 = _each(_dot, a4, a4)
    a3 = _each(_dot, a_d, a2)
    a12 = _each(_dot, a4, a8)
    lo = _each(lambda x1, x2, x3: eye + x1 + x2 + x3, a_d, a2, a3)
    hi = _each(lambda x4, x8, x12: eye + x4 + x8 + x12, a4, a8, a12)
    t = _each(_dot, lo, hi)
    for mask in (off32, off64):
        a_o = _each(lambda a: jnp.where(mask, a, 0.0).astype(BF16), a_list)
        tb = _each(lambda x: x.astype(BF16), t)
        ta = _each(_dot, tb, a_o)
        tat = _each(_dot, ta, tb)
        t = _each(lambda x, y: x + y, t, tat)
    return t


def _scan_kernel(r_ref, lw_ref, k_ref, v_ref, a_ref, b_ref, y_ref, ht_ref,
                 abar_ref, vbar_ref, vbart_ref, kv_ref, yv_ref, grb_ref, rs_ref, bhs_ref, decay_ref,
                 *, n_chunks, n_pairs, chunks_per_trip):
    c = SCAN_CHUNK

    @pl.when(pl.program_id(1) == 0)
    def _():
        ht_ref[...] = jnp.zeros_like(ht_ref)

    row = lax.broadcasted_iota(jnp.int32, (2 * c, 2 * c), 0)
    col = lax.broadcasted_iota(jnp.int32, (2 * c, 2 * c), 1)
    strict = col < row
    incl = col <= row
    tri = (lax.broadcasted_iota(jnp.int32, (c, c), 1)
           <= lax.broadcasted_iota(jnp.int32, (c, c), 0)).astype(F32)
    head0 = lax.broadcasted_iota(jnp.int32, (c, LANES), 1) < RWKV_HEAD

    def stack(x):
        return jnp.concatenate([jnp.where(head0, x, 0.0), jnp.where(head0, 0.0, x)],
                               axis=0).astype(BF16)

    def prepare(trip, carry):
        units = [(trip * chunks_per_trip + dc, g)
                 for dc in range(chunks_per_trip) for g in range(n_pairs)]

        def load(ref):
            return [ref[pl.ds(pl.multiple_of(ci * c, c), c), g * LANES:(g + 1) * LANES]
                    for ci, g in units]

        lw = load(lw_ref)
        cum = _each(lambda x: jnp.dot(tri, x, precision=lax.Precision.HIGHEST,
                                      preferred_element_type=F32), lw)
        total = _each(lambda x: x[c - 1:c, :], cum)
        p_in = _each(jnp.exp, cum)
        p_ex = _each(lambda x, y: jnp.exp(x - y), cum, lw)
        p_inv = _each(lambda x: jnp.exp(-x), cum)
        p_rest = _each(lambda t, x: jnp.exp(t - x), total, cum)
        r, k, v, a, b = load(r_ref), load(k_ref), load(v_ref), load(a_ref), load(b_ref)
        a_s = _each(lambda x, p: stack(x * p), a, p_ex)
        r_s = _each(lambda x, p: stack(x * p), r, p_in)
        b_s = _each(lambda x, p: stack(x * p), b, p_inv)
        k_s = _each(lambda x, p: stack(x * p), k, p_inv)
        bh_s = _each(lambda x, p: stack(x * p), b, p_rest)
        kh_s = _each(lambda x, p: stack(x * p), k, p_rest)
        v_s = _each(stack, v)

        g_ab = _each(lambda x, y: jnp.where(strict, _dot_nt(x, y), 0.0), a_s, b_s)
        g_ak = _each(lambda x, y: jnp.where(strict, _dot_nt(x, y), 0.0).astype(BF16), a_s, k_s)
        g_rb = _each(lambda x, y: jnp.where(incl, _dot_nt(x, y), 0.0).astype(BF16), r_s, b_s)
        g_rk = _each(lambda x, y: jnp.where(incl, _dot_nt(x, y), 0.0).astype(BF16), r_s, k_s)
        gv = _each(_dot, g_ak, v_s)
        kv = _each(_dot_tn, v_s, kh_s)
        yv = _each(_dot, g_rk, v_s)
        t = _each(lambda x: x.astype(BF16), _unit_lower_inverse(g_ab))
        a_bar = _each(_dot, t, a_s)
        v_bar = _each(_dot, t, gv)
        for i, (ci, g) in enumerate(units):
            abar_ref[ci, g] = a_bar[i].astype(BF16)
            vbar_ref[ci, g] = v_bar[i]
            vbart_ref[ci, g] = v_bar[i].T
            kv_ref[ci, g] = kv[i]
            yv_ref[ci, g] = yv[i]
            grb_ref[ci, g] = g_rb[i]
            rs_ref[ci, g] = r_s[i]
            bhs_ref[ci, g] = bh_s[i]
            decay_ref[ci, g] = jnp.exp(total[i])
        return carry

    lax.fori_loop(0, n_chunks // chunks_per_trip, prepare, 0)

    def advance(ci, carry):
        rows = pl.ds(pl.multiple_of(ci * c, c), c)
        pairs = list(range(n_pairs))
        ht = _each(lambda g: ht_ref[g], pairs)
        htb = _each(lambda x: x.astype(BF16), ht)
        a_bar = _each(lambda g: abar_ref[ci, g], pairs)
        u_t = _each(lambda h, ab, g: _dot_nt(h, ab) + vbart_ref[ci, g], htb, a_bar, pairs)
        u = _each(lambda h, ab, g: _dot_nt(ab, h) + vbar_ref[ci, g], htb, a_bar, pairs)
        upd = _each(lambda x, g: _dot(x, bhs_ref[ci, g]), u_t, pairs)
        for g in pairs:
            ht_ref[g] = ht[g] * decay_ref[ci, g] + upd[g] + kv_ref[ci, g]
        y_h = _each(lambda h, g: _dot_nt(rs_ref[ci, g], h), htb, pairs)
        y_u = _each(lambda x, g: _dot(grb_ref[ci, g], x), u, pairs)
        for g in pairs:
            y_s = y_h[g] + y_u[g] + yv_ref[ci, g]
            y_ref[rows, g * LANES:(g + 1) * LANES] = y_s[:c] + y_s[c:]
        return carry

    lax.fori_loop(0, n_chunks, advance, 0)


def _rwkv_scan(r, lw, k, v, a, b):
    s, c = r.shape
    n_pairs = 4 if c % (4 * LANES) == 0 else 1
    wl = n_pairs * LANES
    tt = _tile(s, 512)
    n_chunks = tt // SCAN_CHUNK
    spec = pl.BlockSpec((tt, wl), lambda p, t: (t, p))

    def per_unit(dtype, rows=LANES):
        return pltpu.VMEM((n_chunks, n_pairs, rows, LANES), dtype)

    return pl.pallas_call(
        functools.partial(_scan_kernel, n_chunks=n_chunks, n_pairs=n_pairs,
                          chunks_per_trip=2 if n_chunks % 2 == 0 else 1),
        grid=(c // wl, s // tt),
        in_specs=[spec] * 6,
        out_specs=spec,
        out_shape=jax.ShapeDtypeStruct((s, c), F32),
        scratch_shapes=[
            pltpu.VMEM((n_pairs, LANES, LANES), F32),
            per_unit(BF16), per_unit(F32), per_unit(F32), per_unit(F32), per_unit(F32),
            per_unit(BF16), per_unit(BF16), per_unit(BF16), per_unit(F32, rows=1),
        ],
        compiler_params=_cparams(("parallel", "arbitrary")),
        name="rwkv_scan",
    )(r, lw, k, v, a, b)


def _rwkv_post_kernel(y_ref, bonus_ref, g_ref, w_ref, b_ref, o_ref):
    lane = lax.broadcasted_iota(jnp.int32, (LANES, LANES), 0) // RWKV_HEAD
    lane_t = lax.broadcasted_iota(jnp.int32, (LANES, LANES), 1) // RWKV_HEAD
    ones_bd = (lane == lane_t).astype(BF16)
    y = y_ref[...]
    mu = _head_sum(y, ones_bd) * (1.0 / RWKV_HEAD)
    yc = y - mu
    var = _head_sum(yc * yc, ones_bd) * (1.0 / RWKV_HEAD)
    yn = yc * lax.rsqrt(var + GN_EPS) * w_ref[...] + b_ref[...]
    o_ref[...] = ((yn + bonus_ref[...]) * g_ref[...]).astype(o_ref.dtype)


def _rwkv_post(y, bonus, g, lnx_w, lnx_b):
    s, c = y.shape
    tm = _tile(s, 256)
    tc = _tile(c, 512)
    tile = pl.BlockSpec((tm, tc), lambda i, j: (i, j))
    vec = pl.BlockSpec((1, tc), lambda i, j: (0, j))
    return pl.pallas_call(
        _rwkv_post_kernel,
        grid=(s // tm, c // tc),
        in_specs=[tile, tile, tile, vec, vec],
        out_specs=tile,
        out_shape=jax.ShapeDtypeStruct((s, c), BF16),
        compiler_params=_cparams(("parallel", "arbitrary")),
        name="rwkv_post",
    )(y, bonus, g, lnx_w, lnx_b)


def _rope_table_kernel(pos_ref, inv_ref, sign_ref, cos_out, sin_out):
    ang = pos_ref[...] * inv_ref[...]
    cos_out[...] = jnp.cos(ang)
    sin_out[...] = jnp.sin(ang) * sign_ref[...]


def _rope_tables(pos_col, inv, sign):
    s = pos_col.shape[0]
    tm = _tile(s, 512)
    vec = pl.BlockSpec((1, LANES), lambda i: (0, 0))
    tab = pl.BlockSpec((tm, LANES), lambda i: (i, 0))
    return pl.pallas_call(
        _rope_table_kernel,
        grid=(s // tm,),
        in_specs=[pl.BlockSpec((tm, 1), lambda i: (i, 0)), vec, vec],
        out_specs=[tab, tab],
        out_shape=[jax.ShapeDtypeStruct((s, LANES), F32)] * 2,
        compiler_params=_cparams(("parallel",)),
        name="rope_tables",
    )(pos_col, inv, sign)


def _rope(t, cos, sin):
    return t * cos + pltpu.roll(t, ATTN_HEAD // 2, axis=1) * sin


def _kv_prep_kernel(pk_ref, pv_ref, cos_ref, sin_ref, k_out, vt_out, kmean_out):
    cos = cos_ref[...]
    sin = sin_ref[...]
    pk = pk_ref[...]
    parts = [_rope(pk[:, j * LANES:(j + 1) * LANES], cos, sin) for j in range(pk.shape[1] // LANES)]
    kr = parts[0] if len(parts) == 1 else jnp.concatenate(parts, axis=1)
    k_out[...] = kr.astype(BF16)
    kmean_out[0] = jnp.mean(kr, axis=0, keepdims=True)
    pv = pv_ref[...]
    ones_rows = (lax.broadcasted_iota(jnp.int32, (VT_ROWS - ATTN_HEAD, MOBA_BLOCK), 0) == 0)
    for j in range(pv.shape[1] // LANES):
        vt_out[j, :ATTN_HEAD, :] = pv[:, j * LANES:(j + 1) * LANES].T.astype(BF16)
        vt_out[j, ATTN_HEAD:, :] = ones_rows.astype(BF16)


def _kv_prep(p, cos, sin, k_col, v_col, width):
    s = p.shape[0]
    tw = 4 * LANES if all(n % (4 * LANES) == 0 for n in (width, k_col, v_col)) else 2 * LANES
    assert width % tw == 0 and k_col % tw == 0 and v_col % tw == 0
    k_off, v_off = k_col // tw, v_col // tw
    nb = s // MOBA_BLOCK
    return pl.pallas_call(
        _kv_prep_kernel,
        grid=(nb, width // tw),
        in_specs=[
            pl.BlockSpec((MOBA_BLOCK, tw), lambda i, j: (i, j + k_off)),
            pl.BlockSpec((MOBA_BLOCK, tw), lambda i, j: (i, j + v_off)),
            pl.BlockSpec((MOBA_BLOCK, LANES), lambda i, j: (i, 0)),
            pl.BlockSpec((MOBA_BLOCK, LANES), lambda i, j: (i, 0)),
        ],
        out_specs=[
            pl.BlockSpec((MOBA_BLOCK, tw), lambda i, j: (i, j)),
            pl.BlockSpec((tw // LANES, VT_ROWS, MOBA_BLOCK), lambda i, j: (j, 0, i)),
            pl.BlockSpec((1, 1, tw), lambda i, j: (i, 0, j)),
        ],
        out_shape=[
            jax.ShapeDtypeStruct((s, width), BF16),
            jax.ShapeDtypeStruct((width // ATTN_HEAD, VT_ROWS, s), BF16),
            jax.ShapeDtypeStruct((nb, 1, width), F32),
        ],
        compiler_params=_cparams(("parallel", "arbitrary")),
        name="moba_kv_prep",
    )(p, p, cos, sin)


def _split_bf16(x):
    hi = x.astype(BF16)
    return hi, (x - hi.astype(F32)).astype(BF16)


def _dot_nt_3pass(a, b):
    a_hi, a_lo = _split_bf16(a)
    b_hi, b_lo = _split_bf16(b)
    return _dot_nt(a_hi, b_hi) + _dot_nt(a_hi, b_lo) + _dot_nt(a_lo, b_hi)


def _q_prep_kernel(pq_ref, cos_ref, sin_ref, kmean_ref, q_out, *, nb):
    qb = pl.program_id(0)
    cos = cos_ref[...]
    sin = sin_ref[...]
    rows = _pad_to(nb, 8)
    blk = lax.broadcasted_iota(jnp.int32, (rows, MOBA_BLOCK), 0)
    for j in range(pq_ref.shape[1] // LANES):
        q = _rope(pq_ref[:, j * LANES:(j + 1) * LANES], cos, sin)
        km = kmean_ref[:, 0, j * LANES:(j + 1) * LANES]
        if nb < rows:
            km = jnp.concatenate([km, jnp.zeros((rows - nb, LANES), F32)], axis=0)
        gate = jnp.where(blk < qb, _dot_nt_3pass(km, q), NEG)
        chosen = blk == qb
        for r in range(MOBA_TOPK):
            top = jnp.max(gate, axis=0, keepdims=True)
            first = jnp.min(jnp.where(gate == top, blk, LANES), axis=0, keepdims=True)
            pick = blk == first
            chosen = chosen | (pick & (r < qb))
            gate = jnp.where(pick, NEG, gate)
        bias = jnp.where(chosen, 0.0, NEG)
        if rows < LANES:
            bias = jnp.concatenate([bias, jnp.full((LANES - rows, MOBA_BLOCK), NEG, F32)], axis=0)
        bias = bias.T
        q_out[j] = jnp.concatenate([q * SCORE_SCALE, bias], axis=1).astype(BF16)


def _q_prep(p, cos, sin, kmean, q_off, heads):
    s = p.shape[0]
    nb = s // MOBA_BLOCK
    hg = 2
    assert nb <= LANES and heads % hg == 0 and q_off % hg == 0
    tab = pl.BlockSpec((MOBA_BLOCK, LANES), lambda i, h: (i, 0))
    return pl.pallas_call(
        functools.partial(_q_prep_kernel, nb=nb),
        grid=(nb, heads // hg),
        in_specs=[
            pl.BlockSpec((MOBA_BLOCK, hg * LANES), lambda i, h: (i, h + q_off // hg)),
            tab, tab,
            pl.BlockSpec((nb, 1, hg * LANES), lambda i, h: (0, 0, h)),
        ],
        out_specs=pl.BlockSpec((hg, MOBA_BLOCK, 2 * LANES), lambda i, h: (h, i, 0)),
        out_shape=jax.ShapeDtypeStruct((heads, s, 2 * LANES), BF16),
        compiler_params=_cparams(("parallel", "arbitrary")),
        name="moba_q_prep",
    )(p, cos, sin, kmean)


def _attn_kernel(q_ref, k_ref, hot_ref, vt_ref, o_ref,
                 s0_ref, s1_ref, mx0_ref, mx1_ref, m_ref, acc_ref, *, kt):
    q_blocks = q_ref.shape[1] // MOBA_BLOCK
    first_qb = pl.program_id(1) * q_blocks
    q = q_ref[0]
    blocks_per_tile = kt // MOBA_BLOCK

    def scores(g, s_ref, mx_ref):
        rows = pl.ds(pl.multiple_of(g * kt, kt), kt)
        k_aug = jnp.concatenate([k_ref[rows, :], hot_ref[rows, :]], axis=1)
        s = lax.dot_general(k_aug, q, (((1,), (1,)), ((), ())), preferred_element_type=F32)
        s_ref[...] = s
        mx_ref[...] = jnp.max(s, axis=0, keepdims=True)

    def softmax_pv(g, s, s_max):
        m = m_ref[...]
        m_new = jnp.maximum(m, s_max)
        alpha = jnp.exp2(m - m_new)
        p = jnp.exp2((s - m_new).astype(BF16))
        vt = vt_ref[0, :, pl.ds(pl.multiple_of(g * kt, kt), kt)]
        acc_ref[...] = alpha * acc_ref[...] + jnp.dot(vt, p, preferred_element_type=F32)
        m_ref[...] = m_new

    n_past = first_qb // blocks_per_tile
    m_ref[...] = jnp.full(m_ref.shape, NEG, F32)
    acc_ref[...] = jnp.zeros(acc_ref.shape, F32)
    scores(0, s0_ref, mx0_ref)

    def pair(j, carry):
        g = 2 * j
        scores(g + 1, s1_ref, mx1_ref)
        softmax_pv(g, s0_ref[...], mx0_ref[...])
        scores(g + 2, s0_ref, mx0_ref)
        softmax_pv(g + 1, s1_ref[...], mx1_ref[...])
        return carry

    lax.fori_loop(0, n_past // 2, pair, 0)

    causal = (lax.broadcasted_iota(jnp.int32, (MOBA_BLOCK, MOBA_BLOCK), 0)
              <= lax.broadcasted_iota(jnp.int32, (MOBA_BLOCK, MOBA_BLOCK), 1))

    def last_tile(s_ref):
        for c in range(q_blocks):
            own = pl.ds(pl.multiple_of(((first_qb + c) % blocks_per_tile) * MOBA_BLOCK, MOBA_BLOCK),
                        MOBA_BLOCK)
            cols = slice(c * MOBA_BLOCK, (c + 1) * MOBA_BLOCK)
            s_ref[own, cols] = jnp.where(causal, s_ref[own, cols], NEG)
        s = s_ref[...]
        softmax_pv(n_past, s, jnp.max(s, axis=0, keepdims=True))

    @pl.when(n_past % 2 == 1)
    def _():
        scores(n_past, s1_ref, mx1_ref)
        softmax_pv(n_past - 1, s0_ref[...], mx0_ref[...])
        last_tile(s1_ref)

    @pl.when(n_past % 2 == 0)
    def _():
        last_tile(s0_ref)

    acc = acc_ref[...]
    out_t = acc[:ATTN_HEAD] / acc[ATTN_HEAD:ATTN_HEAD + 1]
    o_ref[...] = out_t.T.astype(o_ref.dtype)


def _moba_attention(q_aug, k, vt):
    heads, s, _ = q_aug.shape
    kt = _tile(s, ATTN_KEY_TILE)
    block_onehot = (jnp.arange(s, dtype=jnp.int32)[:, None] // MOBA_BLOCK
                    == jnp.arange(LANES, dtype=jnp.int32)[None, :]).astype(BF16)
    q_blocks = ATTN_Q_BLOCKS if (kt // MOBA_BLOCK) % ATTN_Q_BLOCKS == 0 else 1
    qw = q_blocks * MOBA_BLOCK
    return pl.pallas_call(
        functools.partial(_attn_kernel, kt=kt),
        grid=(heads, s // qw),
        in_specs=[
            pl.BlockSpec((1, qw, 2 * LANES), lambda h, i: (h, i, 0)),
            pl.BlockSpec((s, ATTN_HEAD), lambda h, i: (0, h)),
            pl.BlockSpec((s, LANES), lambda h, i: (0, 0)),
            pl.BlockSpec((1, VT_ROWS, s), lambda h, i: (h, 0, 0)),
        ],
        out_specs=pl.BlockSpec((qw, ATTN_HEAD), lambda h, i: (i, h)),
        out_shape=jax.ShapeDtypeStruct((s, heads * ATTN_HEAD), BF16),
        scratch_shapes=[
            pltpu.VMEM((kt, qw), F32), pltpu.VMEM((kt, qw), F32),
            pltpu.VMEM((1, qw), F32), pltpu.VMEM((1, qw), F32),
            pltpu.VMEM((1, qw), F32), pltpu.VMEM((VT_ROWS, qw), F32),
        ],
        compiler_params=_cparams(("parallel", "arbitrary")),
        name="moba_attention",
    )(q_aug, k, block_onehot, vt)


def _ffn_block(x, xb, w_gate, w_up, w_down, g, b, alpha, name):
    f = w_gate.shape[1]
    fp = _pad_to(f, 256)
    wg = jnp.pad(w_gate, ((0, 0), (0, fp - f))).astype(BF16)
    wu = jnp.pad(w_up, ((0, 0), (0, fp - f))).astype(BF16)
    wd = jnp.pad(w_down, ((0, fp - f), (0, 0))).astype(BF16)
    (h,) = _matmul([xb], [wg, wu], [(0, 0), (0, 1)], _swiglu_epilogue, [BF16],
                   tm=1024, tn=256, name=name + "_up")
    (y,) = _matmul([h], [wd], [(0, 0)], _identity_epilogue, [F32],
                   tm=512, tn=512, name=name + "_down")
    return _residual_ln(x, y, g, b, alpha, 0.5, name + "_ln")


def _token_mixing(x1b, positions, w_in, shift_mix, decay_w0, decay_up, iclr_a0, iclr_up, gate_up,
                  k_k, k_a, r_k, lnx_w, lnx_b, w_o_rwkv, w_o_attn, w_out):
    s, d = x1b.shape
    c = decay_w0.shape[0]
    gate_rank = gate_up.shape[0]
    decay_rank, iclr_rank = decay_up.shape[0], iclr_up.shape[0]
    rwkv_in = 3 * c + decay_rank + iclr_rank + gate_rank
    cw = w_o_attn.shape[0]
    heads = cw // ATTN_HEAD

    lora_w = _pad_to(decay_rank + iclr_rank + gate_rank, 256)
    pad = lora_w - (decay_rank + iclr_rank + gate_rank)
    w_rwkv = jnp.pad(w_in[:, :rwkv_in].astype(BF16), ((0, 0), (0, pad)))
    w_attn = w_in[:, rwkv_in:].astype(BF16)
    mix_p = jnp.pad(shift_mix, (0, pad)).reshape(1, -1)
    gate_up_p = jnp.pad(gate_up, ((0, pad), (0, 0))).astype(BF16)

    (p_rwkv,) = _matmul([x1b], [w_rwkv], [(0, 0)], _identity_epilogue, [F32],
                        tm=2048, tn=256, name="in_proj_rwkv")
    (p_attn,) = _matmul([x1b], [w_attn], [(0, 0)], _identity_epilogue, [F32],
                        tm=2048, tn=256, name="in_proj_attn")

    r, lw, k_mod, v, a, b, g, bonus = _rwkv_prep(
        p_rwkv, c, lora_w, mix_p, decay_w0.reshape(1, c), iclr_a0.reshape(1, c), k_k.reshape(1, c),
        k_a.reshape(1, c), r_k.reshape(1, c), decay_up.astype(BF16), iclr_up.astype(BF16), gate_up_p)
    y = _rwkv_scan(r, lw, k_mod, v, a, b)
    yg = _rwkv_post(y, bonus, g, lnx_w.reshape(1, c), lnx_b.reshape(1, c))

    half = ATTN_HEAD // 2
    inv_half = ROPE_THETA ** (-jnp.arange(0, ATTN_HEAD, 2, dtype=F32) / ATTN_HEAD)
    inv = jnp.concatenate([inv_half, inv_half]).reshape(1, ATTN_HEAD)
    sign = jnp.concatenate([-jnp.ones((half,), F32), jnp.ones((half,), F32)]).reshape(1, ATTN_HEAD)
    pos_col = positions.astype(F32).reshape(s, 1)
    assert cw % (2 * LANES) == 0
    cos, sin = _rope_tables(pos_col, inv, sign)
    k_rope, v_b, kmean = _kv_prep(p_attn, cos, sin, cw, 2 * cw, cw)
    q_aug = _q_prep(p_attn, cos, sin, kmean, 0, heads)
    attn = _moba_attention(q_aug, k_rope, v_b)

    gate_off = 3 * cw // 256
    (merged,) = _matmul([yg, attn], [w_o_rwkv.astype(BF16), w_o_attn.astype(BF16)], [(0, 0), (1, 1)],
                        _gated_merge_epilogue, [BF16], tm=1024, tn=256,
                        extras=[(p_attn, gate_off), (p_attn, gate_off + d // 256)], name="branch_out")
    (mix,) = _matmul([merged], [w_out.astype(BF16)], [(0, 0)], _identity_epilogue, [F32],
                     tm=2048, tn=256, name="mix_out")
    return mix


def kernel(x, positions, ffn1_w_gate, ffn1_w_up, ffn1_w_down, ln1_g, ln1_b, w_in, shift_mix, decay_w0, decay_up, iclr_a0, iclr_up, gate_up, k_k, k_a, r_k, lnx_w, lnx_b, w_o_rwkv, w_o_attn, w_out, ln2_g, ln2_b, ffn2_w_gate, ffn2_w_up, ffn2_w_down, ln3_g, ln3_b):
    bsz, s, d = x.shape
    depth = w_in.shape[0]
    alpha = (2 * depth) ** 0.25
    outs = []
    for bi in range(bsz):
        xf = x.reshape(s, d) if bsz == 1 else x[bi]
        xb = xf.astype(BF16)
        for l in range(depth):
            xf, xb = _ffn_block(xf, xb, ffn1_w_gate[l], ffn1_w_up[l], ffn1_w_down[l],
                                ln1_g[l], ln1_b[l], alpha, "ffn1")
            mix = _token_mixing(xb, positions[bi], w_in[l], shift_mix[l], decay_w0[l], decay_up[l],
                                iclr_a0[l], iclr_up[l], gate_up[l], k_k[l], k_a[l], r_k[l],
                                lnx_w[l], lnx_b[l], w_o_rwkv[l], w_o_attn[l], w_out[l])
            xf, xb = _residual_ln(xf, mix, ln2_g[l], ln2_b[l], alpha, 1.0, "ln2")
            xf, xb = _ffn_block(xf, xb, ffn2_w_gate[l], ffn2_w_up[l], ffn2_w_down[l],
                                ln3_g[l], ln3_b[l], alpha, "ffn2")
        outs.append(xf)
    return outs[0].reshape(1, s, d) if bsz == 1 else jnp.stack(outs)
```

```python
import functools

import jax
import jax.numpy as jnp
from jax import lax
from jax.experimental import pallas as pl
from jax.experimental.pallas import tpu as pltpu

F32 = jnp.float32
BF16 = jnp.bfloat16

LN_EPS = 1e-5
GN_EPS = 64e-5
RWKV_HEAD = 64
ATTN_HEAD = 128
MOBA_BLOCK = 256
MOBA_TOPK = 3
ROPE_THETA = 10000.0
NEG = -1e30

SCORE_SCALE = ATTN_HEAD ** -0.5 * 1.4426950408889634
ATTN_KEY_TILE = 1024
ATTN_Q_BLOCKS = 4

VT_ROWS = ATTN_HEAD + 16

LANES = 128
SCAN_CHUNK = 64
VMEM_LIMIT = 56 * 1024 * 1024


def _cparams(sem):
    return pltpu.CompilerParams(dimension_semantics=sem, vmem_limit_bytes=VMEM_LIMIT)


def _pad_to(n, m):
    return (n + m - 1) // m * m


def _tile(n, pref):
    t = min(n, pref)
    assert n % t == 0, (n, t)
    return t


def _mm_kernel(*refs, na, nb, ne, dots, epilogue):
    a_refs = refs[:na]
    b_refs = refs[na:na + nb]
    e_refs = refs[na + nb:na + nb + ne]
    o_refs = refs[na + nb + ne:]
    accs = [jnp.dot(a_refs[i][...], b_refs[j][...], preferred_element_type=F32) for i, j in dots]
    outs = epilogue(accs, [e[...] for e in e_refs])
    for o, val in zip(o_refs, outs):
        o[...] = val.astype(o.dtype)


def _matmul(a_list, b_list, dots, epilogue, out_dtypes, *, tm, tn, extras=(), name):
    m = a_list[0].shape[0]
    n = b_list[0].shape[1]
    tm = _tile(m, tm)
    tn = _tile(n, tn)
    in_specs = [pl.BlockSpec((tm, a.shape[1]), lambda i, j: (i, 0)) for a in a_list]
    in_specs += [pl.BlockSpec((b.shape[0], tn), lambda i, j: (0, j)) for b in b_list]
    for arr, off in extras:
        if arr.shape[0] == 1:
            in_specs.append(pl.BlockSpec((1, tn), lambda i, j, off=off: (0, j + off)))
        else:
            in_specs.append(pl.BlockSpec((tm, tn), lambda i, j, off=off: (i, j + off)))
    kern = functools.partial(_mm_kernel, na=len(a_list), nb=len(b_list), ne=len(extras),
                             dots=dots, epilogue=epilogue)
    outs = pl.pallas_call(
        kern,
        grid=(m // tm, n // tn),
        in_specs=in_specs,
        out_specs=[pl.BlockSpec((tm, tn), lambda i, j: (i, j)) for _ in out_dtypes],
        out_shape=[jax.ShapeDtypeStruct((m, n), dt) for dt in out_dtypes],
        compiler_params=_cparams(("parallel", "arbitrary")),
        name=name,
    )(*a_list, *b_list, *[e for e, _ in extras])
    return outs


def _sigmoid(x):
    return 0.5 * jnp.tanh(0.5 * x) + 0.5


def _swiglu_epilogue(accs, _):
    g, u = accs
    return [g * _sigmoid(g) * u]


def _identity_epilogue(accs, _):
    return [accs[0]]


def _gated_merge_epilogue(accs, extras):
    yr, ya = accs
    pr, pa = extras
    return [_sigmoid(pr) * yr + _sigmoid(pa) * ya]


def _ln_kernel(x_ref, y_ref, g_ref, b_ref, o_ref, ob_ref, *, alpha, beta):
    z = alpha * x_ref[...] + beta * y_ref[...]
    mu = jnp.mean(z, axis=-1, keepdims=True)
    zc = z - mu
    var = jnp.mean(zc * zc, axis=-1, keepdims=True)
    out = zc * lax.rsqrt(var + LN_EPS) * g_ref[...] + b_ref[...]
    o_ref[...] = out
    ob_ref[...] = out.astype(BF16)


def _residual_ln(x, y, g, b, alpha, beta, name):
    s, d = x.shape
    tm = _tile(s, 256)
    row = pl.BlockSpec((tm, d), lambda i: (i, 0))
    vec = pl.BlockSpec((1, d), lambda i: (0, 0))
    return pl.pallas_call(
        functools.partial(_ln_kernel, alpha=alpha, beta=beta),
        grid=(s // tm,),
        in_specs=[row, row, vec, vec],
        out_specs=[row, row],
        out_shape=[jax.ShapeDtypeStruct((s, d), F32), jax.ShapeDtypeStruct((s, d), BF16)],
        compiler_params=_cparams(("parallel",)),
        name=name,
    )(x, y, g.reshape(1, d), b.reshape(1, d))


def _head_sum(x, ones_bd):
    hi = x.astype(BF16)
    rest = x - hi.astype(F32)
    mid = rest.astype(BF16)
    lo = (rest - mid.astype(F32)).astype(BF16)
    parts = []
    for j in range(x.shape[1] // LANES):
        lanes = slice(j * LANES, (j + 1) * LANES)
        parts.append(jnp.dot(hi[:, lanes], ones_bd, preferred_element_type=F32)
                     + jnp.dot(mid[:, lanes], ones_bd, preferred_element_type=F32)
                     + jnp.dot(lo[:, lanes], ones_bd, preferred_element_type=F32))
    return parts[0] if len(parts) == 1 else jnp.concatenate(parts, axis=1)


def _shifted(cur, prev8, mix, first):
    rolled = pltpu.roll(cur, 1, axis=0)
    last = jnp.where(first, 0.0, prev8[7:8, :])
    row0 = lax.broadcasted_iota(jnp.int32, cur.shape, 0) == 0
    prev = jnp.where(row0, last, rolled)
    return cur + (prev - cur) * mix


def _rwkv_prep_kernel(pr_ref, pk_ref, pv_ref, pl_ref, pr8_ref, pk8_ref, pv8_ref, pl8_ref,
                      mr_ref, mk_ref, mv_ref, ml_ref,
                      w0_ref, a0_ref, kk_ref, ka_ref, rk_ref,
                      dup_ref, iup_ref, gup_ref,
                      r_out, lw_out, k_out, v_out, a_out, b_out, g_out, bonus_out, act_ref,
                      *, decay_rank, iclr_rank):
    first = pl.program_id(0) == 0
    r = _shifted(pr_ref[...], pr8_ref[...], mr_ref[...], first)
    k = _shifted(pk_ref[...], pk8_ref[...], mk_ref[...], first)
    v = _shifted(pv_ref[...], pv8_ref[...], mv_ref[...], first)

    @pl.when(pl.program_id(1) == 0)
    def _():
        zl = _shifted(pl_ref[...], pl8_ref[...], ml_ref[...], first)
        act_ref[:, :decay_rank] = jnp.tanh(zl[:, :decay_rank]).astype(BF16)
        act_ref[:, decay_rank:decay_rank + iclr_rank] = (
            zl[:, decay_rank:decay_rank + iclr_rank].astype(BF16))
        act_ref[:, decay_rank + iclr_rank:] = _sigmoid(zl[:, decay_rank + iclr_rank:]).astype(BF16)

    lw = jnp.dot(act_ref[:, :decay_rank], dup_ref[...], preferred_element_type=F32)
    la = jnp.dot(act_ref[:, decay_rank:decay_rank + iclr_rank], iup_ref[...],
                 preferred_element_type=F32)
    g = jnp.dot(act_ref[:, decay_rank + iclr_rank:], gup_ref[...], preferred_element_type=F32)

    u = -(w0_ref[...] + lw)
    softplus = jnp.maximum(u, 0.0) + jnp.log(1.0 + jnp.exp(-jnp.abs(u)))
    w_log = -softplus - 0.5
    log_decay = -jnp.exp(w_log)
    a_sig = _sigmoid(a0_ref[...] + la)

    lane = lax.broadcasted_iota(jnp.int32, (LANES, LANES), 0) // RWKV_HEAD
    lane_t = lax.broadcasted_iota(jnp.int32, (LANES, LANES), 1) // RWKV_HEAD
    ones_bd = (lane == lane_t).astype(BF16)

    kk = k * kk_ref[...]
    kk = kk * lax.rsqrt(jnp.maximum(_head_sum(kk * kk, ones_bd), 1e-24))
    k_mod = k * (1.0 + (a_sig - 1.0) * ka_ref[...])
    bonus = _head_sum(r * k_mod * rk_ref[...], ones_bd) * v

    r_out[...] = r.astype(r_out.dtype)
    lw_out[...] = log_decay
    k_out[...] = k_mod.astype(k_out.dtype)
    v_out[...] = v.astype(v_out.dtype)
    a_out[...] = (-kk).astype(a_out.dtype)
    b_out[...] = (kk * a_sig).astype(b_out.dtype)
    g_out[...] = g.astype(g_out.dtype)
    bonus_out[...] = bonus.astype(bonus_out.dtype)


def _rwkv_prep(p, c, lora_w, shift_mix_p, decay_w0, iclr_a0, k_k, k_a, r_k_flat,
               decay_up, iclr_up, gate_up_p):
    s = p.shape[0]
    tm = _tile(s, 256)
    tc = _tile(c, 512)
    ncb = c // tc
    assert (3 * c) % lora_w == 0 and tm % 8 == 0
    lora_blk = 3 * c // lora_w
    decay_rank, iclr_rank = decay_up.shape[0], iclr_up.shape[0]

    def col(off):
        return pl.BlockSpec((tm, tc), lambda i, j: (i, j + off))

    def col8(off):
        return pl.BlockSpec((8, tc), lambda i, j: (jnp.maximum(i * (tm // 8) - 1, 0), j + off))

    def vec(off):
        return pl.BlockSpec((1, tc), lambda i, j: (0, j + off))

    in_specs = [
        col(0), col(ncb), col(2 * ncb),
        pl.BlockSpec((tm, lora_w), lambda i, j: (i, lora_blk)),
        col8(0), col8(ncb), col8(2 * ncb),
        pl.BlockSpec((8, lora_w), lambda i, j: (jnp.maximum(i * (tm // 8) - 1, 0), lora_blk)),
        vec(0), vec(ncb), vec(2 * ncb),
        pl.BlockSpec((1, lora_w), lambda i, j: (0, lora_blk)),
        vec(0), vec(0), vec(0), vec(0), vec(0),
        pl.BlockSpec((decay_rank, tc), lambda i, j: (0, j)),
        pl.BlockSpec((iclr_rank, tc), lambda i, j: (0, j)),
        pl.BlockSpec((gate_up_p.shape[0], tc), lambda i, j: (0, j)),
    ]
    out_spec = pl.BlockSpec((tm, tc), lambda i, j: (i, j))
    outs = pl.pallas_call(
        functools.partial(_rwkv_prep_kernel, decay_rank=decay_rank, iclr_rank=iclr_rank),
        grid=(s // tm, ncb),
        in_specs=in_specs,
        out_specs=[out_spec] * 8,
        out_shape=[jax.ShapeDtypeStruct((s, c), F32 if i == 1 else BF16) for i in range(8)],
        scratch_shapes=[pltpu.VMEM((tm, lora_w), BF16)],
        compiler_params=_cparams(("arbitrary", "arbitrary")),
        name="rwkv_prep",
    )(p, p, p, p, p, p, p, p,
      shift_mix_p, shift_mix_p, shift_mix_p, shift_mix_p,
      decay_w0, iclr_a0, k_k, k_a, r_k_flat,
      decay_up, iclr_up, gate_up_p)
    return outs


def _dot(a, b):
    return jnp.dot(a.astype(BF16), b.astype(BF16), preferred_element_type=F32)


def _dot_nt(a, b):
    return lax.dot_general(a.astype(BF16), b.astype(BF16), (((1,), (1,)), ((), ())),
                           preferred_element_type=F32)


def _dot_tn(a, b):
    return lax.dot_general(a.astype(BF16), b.astype(BF16), (((0,), (0,)), ((), ())),
                           preferred_element_type=F32)


def _each(fn, *lists):
    return [fn(*args) for args in zip(*lists)]


def _unit_lower_inverse(a_list):
    n = a_list[0].shape[0]
    row = lax.broadcasted_iota(jnp.int32, (n, n), 0)
    col = lax.broadcasted_iota(jnp.int32, (n, n), 1)
    eye = (row == col).astype(F32)
    same16 = (row // 16) == (col // 16)
    same32 = (row // 32) == (col // 32)
    same64 = (row // 64) == (col // 64)
    off32 = same32 & jnp.logical_not(same16)
    off64 = same64 & jnp.logical_not(same32)
    a_d = _each(lambda a: jnp.where(same16, a, 0.0).astype(BF16), a_list)
    a2 = _each(_dot, a_d, a_d)
    a4 = _each(_dot, a2, a2)
    a8 = _each(_dot, a4, a4)
    a3 = _each(_dot, a_d, a2)
    a12 = _each(_dot, a4, a8)
    lo = _each(lambda x1, x2, x3: eye + x1 + x2 + x3, a_d, a2, a3)
    hi = _each(lambda x4, x8, x12: eye + x4 + x8 + x12, a4, a8, a12)
    t = _each(_dot, lo, hi)
    for mask in (off32, off64):
        a_o = _each(lambda a: jnp.where(mask, a, 0.0).astype(BF16), a_list)
        tb = _each(lambda x: x.astype(BF16), t)
        ta = _each(_dot, tb, a_o)
        tat = _each(_dot, ta, tb)
        t = _each(lambda x, y: x + y, t, tat)
    return t


def _scan_kernel(r_ref, lw_ref, k_ref, v_ref, a_ref, b_ref, y_ref, ht_ref,
                 abar_ref, vbar_ref, vbart_ref, kv_ref, yv_ref, grb_ref, rs_ref, bhs_ref, decay_ref,
                 *, n_chunks, n_pairs, chunks_per_trip):
    c = SCAN_CHUNK

    @pl.when(pl.program_id(1) == 0)
    def _():
        ht_ref[...] = jnp.zeros_like(ht_ref)

    row = lax.broadcasted_iota(jnp.int32, (2 * c, 2 * c), 0)
    col = lax.broadcasted_iota(jnp.int32, (2 * c, 2 * c), 1)
    strict = col < row
    incl = col <= row
    tri = (lax.broadcasted_iota(jnp.int32, (c, c), 1)
           <= lax.broadcasted_iota(jnp.int32, (c, c), 0)).astype(F32)
    head0 = lax.broadcasted_iota(jnp.int32, (c, LANES), 1) < RWKV_HEAD

    def stack(x):
        return jnp.concatenate([jnp.where(head0, x, 0.0), jnp.where(head0, 0.0, x)],
                               axis=0).astype(BF16)

    def prepare(trip, carry):
        units = [(trip * chunks_per_trip + dc, g)
                 for dc in range(chunks_per_trip) for g in range(n_pairs)]

        def load(ref):
            return [ref[pl.ds(pl.multiple_of(ci * c, c), c), g * LANES:(g + 1) * LANES]
                    for ci, g in units]

        lw = load(lw_ref)
        cum = _each(lambda x: jnp.dot(tri, x, precision=lax.Precision.HIGHEST,
                                      preferred_element_type=F32), lw)
        total = _each(lambda x: x[c - 1:c, :], cum)
        p_in = _each(jnp.exp, cum)
        p_ex = _each(lambda x, y: jnp.exp(x - y), cum, lw)
        p_inv = _each(lambda x: jnp.exp(-x), cum)
        p_rest = _each(lambda t, x: jnp.exp(t - x), total, cum)
        r, k, v, a, b = load(r_ref), load(k_ref), load(v_ref), load(a_ref), load(b_ref)
        a_s = _each(lambda x, p: stack(x * p), a, p_ex)
        r_s = _each(lambda x, p: stack(x * p), r, p_in)
        b_s = _each(lambda x, p: stack(x * p), b, p_inv)
        k_s = _each(lambda x, p: stack(x * p), k, p_inv)
        bh_s = _each(lambda x, p: stack(x * p), b, p_rest)
        kh_s = _each(lambda x, p: stack(x * p), k, p_rest)
        v_s = _each(stack, v)

        g_ab = _each(lambda x, y: jnp.where(strict, _dot_nt(x, y), 0.0), a_s, b_s)
        g_ak = _each(lambda x, y: jnp.where(strict, _dot_nt(x, y), 0.0).astype(BF16), a_s, k_s)
        g_rb = _each(lambda x, y: jnp.where(incl, _dot_nt(x, y), 0.0).astype(BF16), r_s, b_s)
        g_rk = _each(lambda x, y: jnp.where(incl, _dot_nt(x, y), 0.0).astype(BF16), r_s, k_s)
        gv = _each(_dot, g_ak, v_s)
        kv = _each(_dot_tn, v_s, kh_s)
        yv = _each(_dot, g_rk, v_s)
        t = _each(lambda x: x.astype(BF16), _unit_lower_inverse(g_ab))
        a_bar = _each(_dot, t, a_s)
        v_bar = _each(_dot, t, gv)
        for i, (ci, g) in enumerate(units):
            abar_ref[ci, g] = a_bar[i].astype(BF16)
            vbar_ref[ci, g] = v_bar[i]
            vbart_ref[ci, g] = v_bar[i].T
            kv_ref[ci, g] = kv[i]
            yv_ref[ci, g] = yv[i]
            grb_ref[ci, g] = g_rb[i]
            rs_ref[ci, g] = r_s[i]
            bhs_ref[ci, g] = bh_s[i]
            decay_ref[ci, g] = jnp.exp(total[i])
        return carry

    lax.fori_loop(0, n_chunks // chunks_per_trip, prepare, 0)

    def advance(ci, carry):
        rows = pl.ds(pl.multiple_of(ci * c, c), c)
        pairs = list(range(n_pairs))
        ht = _each(lambda g: ht_ref[g], pairs)
        htb = _each(lambda x: x.astype(BF16), ht)
        a_bar = _each(lambda g: abar_ref[ci, g], pairs)
        u_t = _each(lambda h, ab, g: _dot_nt(h, ab) + vbart_ref[ci, g], htb, a_bar, pairs)
        u = _each(lambda h, ab, g: _dot_nt(ab, h) + vbar_ref[ci, g], htb, a_bar, pairs)
        upd = _each(lambda x, g: _dot(x, bhs_ref[ci, g]), u_t, pairs)
        for g in pairs:
            ht_ref[g] = ht[g] * decay_ref[ci, g] + upd[g] + kv_ref[ci, g]
        y_h = _each(lambda h, g: _dot_nt(rs_ref[ci, g], h), htb, pairs)
        y_u = _each(lambda x, g: _dot(grb_ref[ci, g], x), u, pairs)
        for g in pairs:
            y_s = y_h[g] + y_u[g] + yv_ref[ci, g]
            y_ref[rows, g * LANES:(g + 1) * LANES] = y_s[:c] + y_s[c:]
        return carry

    lax.fori_loop(0, n_chunks, advance, 0)


def _rwkv_scan(r, lw, k, v, a, b):
    s, c = r.shape
    n_pairs = 4 if c % (4 * LANES) == 0 else 1
    wl = n_pairs * LANES
    tt = _tile(s, 512)
    n_chunks = tt // SCAN_CHUNK
    spec = pl.BlockSpec((tt, wl), lambda p, t: (t, p))

    def per_unit(dtype, rows=LANES):
        return pltpu.VMEM((n_chunks, n_pairs, rows, LANES), dtype)

    return pl.pallas_call(
        functools.partial(_scan_kernel, n_chunks=n_chunks, n_pairs=n_pairs,
                          chunks_per_trip=next(n for n in (4, 2, 1) if n_chunks % n == 0)),
        grid=(c // wl, s // tt),
        in_specs=[spec] * 6,
        out_specs=spec,
        out_shape=jax.ShapeDtypeStruct((s, c), F32),
        scratch_shapes=[
            pltpu.VMEM((n_pairs, LANES, LANES), F32),
            per_unit(BF16), per_unit(F32), per_unit(F32), per_unit(F32), per_unit(F32),
            per_unit(BF16), per_unit(BF16), per_unit(BF16), per_unit(F32, rows=1),
        ],
        compiler_params=_cparams(("parallel", "arbitrary")),
        name="rwkv_scan",
    )(r, lw, k, v, a, b)


def _rwkv_post_kernel(y_ref, bonus_ref, g_ref, w_ref, b_ref, o_ref):
    lane = lax.broadcasted_iota(jnp.int32, (LANES, LANES), 0) // RWKV_HEAD
    lane_t = lax.broadcasted_iota(jnp.int32, (LANES, LANES), 1) // RWKV_HEAD
    ones_bd = (lane == lane_t).astype(BF16)
    y = y_ref[...]
    mu = _head_sum(y, ones_bd) * (1.0 / RWKV_HEAD)
    yc = y - mu
    var = _head_sum(yc * yc, ones_bd) * (1.0 / RWKV_HEAD)
    yn = yc * lax.rsqrt(var + GN_EPS) * w_ref[...] + b_ref[...]
    o_ref[...] = ((yn + bonus_ref[...]) * g_ref[...]).astype(o_ref.dtype)


def _rwkv_post(y, bonus, g, lnx_w, lnx_b):
    s, c = y.shape
    tm = _tile(s, 256)
    tc = _tile(c, 512)
    tile = pl.BlockSpec((tm, tc), lambda i, j: (i, j))
    vec = pl.BlockSpec((1, tc), lambda i, j: (0, j))
    return pl.pallas_call(
        _rwkv_post_kernel,
        grid=(s // tm, c // tc),
        in_specs=[tile, tile, tile, vec, vec],
        out_specs=tile,
        out_shape=jax.ShapeDtypeStruct((s, c), BF16),
        compiler_params=_cparams(("parallel", "arbitrary")),
        name="rwkv_post",
    )(y, bonus, g, lnx_w, lnx_b)


def _rope_table_kernel(pos_ref, inv_ref, sign_ref, cos_out, sin_out):
    ang = pos_ref[...] * inv_ref[...]
    cos_out[...] = jnp.cos(ang)
    sin_out[...] = jnp.sin(ang) * sign_ref[...]


def _rope_tables(pos_col, inv, sign):
    s = pos_col.shape[0]
    tm = _tile(s, 512)
    vec = pl.BlockSpec((1, LANES), lambda i: (0, 0))
    tab = pl.BlockSpec((tm, LANES), lambda i: (i, 0))
    return pl.pallas_call(
        _rope_table_kernel,
        grid=(s // tm,),
        in_specs=[pl.BlockSpec((tm, 1), lambda i: (i, 0)), vec, vec],
        out_specs=[tab, tab],
        out_shape=[jax.ShapeDtypeStruct((s, LANES), F32)] * 2,
        compiler_params=_cparams(("parallel",)),
        name="rope_tables",
    )(pos_col, inv, sign)


def _rope(t, cos, sin):
    return t * cos + pltpu.roll(t, ATTN_HEAD // 2, axis=1) * sin


def _kv_prep_kernel(pk_ref, pv_ref, cos_ref, sin_ref, k_out, vt_out, kmean_out):
    cos = cos_ref[...]
    sin = sin_ref[...]
    pk = pk_ref[...]
    parts = [_rope(pk[:, j * LANES:(j + 1) * LANES], cos, sin) for j in range(pk.shape[1] // LANES)]
    kr = parts[0] if len(parts) == 1 else jnp.concatenate(parts, axis=1)
    k_out[...] = kr.astype(BF16)
    kmean_out[0] = jnp.mean(kr, axis=0, keepdims=True)
    pv = pv_ref[...]
    ones_rows = (lax.broadcasted_iota(jnp.int32, (VT_ROWS - ATTN_HEAD, MOBA_BLOCK), 0) == 0)
    for j in range(pv.shape[1] // LANES):
        vt_out[j, :ATTN_HEAD, :] = pv[:, j * LANES:(j + 1) * LANES].T.astype(BF16)
        vt_out[j, ATTN_HEAD:, :] = ones_rows.astype(BF16)


def _kv_prep(p, cos, sin, k_col, v_col, width):
    s = p.shape[0]
    tw = 4 * LANES if all(n % (4 * LANES) == 0 for n in (width, k_col, v_col)) else 2 * LANES
    assert width % tw == 0 and k_col % tw == 0 and v_col % tw == 0
    k_off, v_off = k_col // tw, v_col // tw
    nb = s // MOBA_BLOCK
    return pl.pallas_call(
        _kv_prep_kernel,
        grid=(nb, width // tw),
        in_specs=[
            pl.BlockSpec((MOBA_BLOCK, tw), lambda i, j: (i, j + k_off)),
            pl.BlockSpec((MOBA_BLOCK, tw), lambda i, j: (i, j + v_off)),
            pl.BlockSpec((MOBA_BLOCK, LANES), lambda i, j: (i, 0)),
            pl.BlockSpec((MOBA_BLOCK, LANES), lambda i, j: (i, 0)),
        ],
        out_specs=[
            pl.BlockSpec((MOBA_BLOCK, tw), lambda i, j: (i, j)),
            pl.BlockSpec((tw // LANES, VT_ROWS, MOBA_BLOCK), lambda i, j: (j, 0, i)),
            pl.BlockSpec((1, 1, tw), lambda i, j: (i, 0, j)),
        ],
        out_shape=[
            jax.ShapeDtypeStruct((s, width), BF16),
            jax.ShapeDtypeStruct((width // ATTN_HEAD, VT_ROWS, s), BF16),
            jax.ShapeDtypeStruct((nb, 1, width), F32),
        ],
        compiler_params=_cparams(("parallel", "arbitrary")),
        name="moba_kv_prep",
    )(p, p, cos, sin)


def _split_bf16(x):
    hi = x.astype(BF16)
    return hi, (x - hi.astype(F32)).astype(BF16)


def _dot_nt_3pass(a, b):
    a_hi, a_lo = _split_bf16(a)
    b_hi, b_lo = _split_bf16(b)
    return _dot_nt(a_hi, b_hi) + _dot_nt(a_hi, b_lo) + _dot_nt(a_lo, b_hi)


def _q_prep_kernel(pq_ref, cos_ref, sin_ref, kmean_ref, q_out, *, nb):
    qb = pl.program_id(0)
    cos = cos_ref[...]
    sin = sin_ref[...]
    rows = _pad_to(nb, 8)
    blk = lax.broadcasted_iota(jnp.int32, (rows, MOBA_BLOCK), 0)
    for j in range(pq_ref.shape[1] // LANES):
        q = _rope(pq_ref[:, j * LANES:(j + 1) * LANES], cos, sin)
        km = kmean_ref[:, 0, j * LANES:(j + 1) * LANES]
        if nb < rows:
            km = jnp.concatenate([km, jnp.zeros((rows - nb, LANES), F32)], axis=0)
        gate = jnp.where(blk < qb, _dot_nt_3pass(km, q), NEG)
        chosen = blk == qb
        for r in range(MOBA_TOPK):
            top = jnp.max(gate, axis=0, keepdims=True)
            first = jnp.min(jnp.where(gate == top, blk, LANES), axis=0, keepdims=True)
            pick = blk == first
            chosen = chosen | (pick & (r < qb))
            gate = jnp.where(pick, NEG, gate)
        bias = jnp.where(chosen, 0.0, NEG)
        if rows < LANES:
            bias = jnp.concatenate([bias, jnp.full((LANES - rows, MOBA_BLOCK), NEG, F32)], axis=0)
        bias = bias.T
        q_out[j] = jnp.concatenate([q * SCORE_SCALE, bias], axis=1).astype(BF16)


def _q_prep(p, cos, sin, kmean, q_off, heads):
    s = p.shape[0]
    nb = s // MOBA_BLOCK
    hg = 2
    assert nb <= LANES and heads % hg == 0 and q_off % hg == 0
    tab = pl.BlockSpec((MOBA_BLOCK, LANES), lambda i, h: (i, 0))
    return pl.pallas_call(
        functools.partial(_q_prep_kernel, nb=nb),
        grid=(nb, heads // hg),
        in_specs=[
            pl.BlockSpec((MOBA_BLOCK, hg * LANES), lambda i, h: (i, h + q_off // hg)),
            tab, tab,
            pl.BlockSpec((nb, 1, hg * LANES), lambda i, h: (0, 0, h)),
        ],
        out_specs=pl.BlockSpec((hg, MOBA_BLOCK, 2 * LANES), lambda i, h: (h, i, 0)),
        out_shape=jax.ShapeDtypeStruct((heads, s, 2 * LANES), BF16),
        compiler_params=_cparams(("parallel", "arbitrary")),
        name="moba_q_prep",
    )(p, cos, sin, kmean)


def _attn_kernel(q_ref, k_ref, hot_ref, vt_ref, o_ref,
                 s0_ref, s1_ref, mx0_ref, mx1_ref, m_ref, acc_ref, *, kt):
    q_blocks = q_ref.shape[1] // MOBA_BLOCK
    first_qb = pl.program_id(1) * q_blocks
    q = q_ref[0]
    blocks_per_tile = kt // MOBA_BLOCK

    def scores(g, s_ref, mx_ref):
        rows = pl.ds(pl.multiple_of(g * kt, kt), kt)
        k_aug = jnp.concatenate([k_ref[rows, :], hot_ref[rows, :]], axis=1)
        s = lax.dot_general(k_aug, q, (((1,), (1,)), ((), ())), preferred_element_type=F32)
        s_ref[...] = s
        mx_ref[...] = jnp.max(s, axis=0, keepdims=True)

    def softmax_pv(g, s, s_max):
        m = m_ref[...]
        m_new = jnp.maximum(m, s_max)
        alpha = jnp.exp2(m - m_new)
        p = jnp.exp2((s - m_new).astype(BF16))
        vt = vt_ref[0, :, pl.ds(pl.multiple_of(g * kt, kt), kt)]
        acc_ref[...] = alpha * acc_ref[...] + jnp.dot(vt, p, preferred_element_type=F32)
        m_ref[...] = m_new

    n_past = first_qb // blocks_per_tile
    m_ref[...] = jnp.full(m_ref.shape, NEG, F32)
    acc_ref[...] = jnp.zeros(acc_ref.shape, F32)
    scores(0, s0_ref, mx0_ref)

    def pair(j, carry):
        g = 2 * j
        scores(g + 1, s1_ref, mx1_ref)
        softmax_pv(g, s0_ref[...], mx0_ref[...])
        scores(g + 2, s0_ref, mx0_ref)
        softmax_pv(g + 1, s1_ref[...], mx1_ref[...])
        return carry

    lax.fori_loop(0, n_past // 2, pair, 0)

    causal = (lax.broadcasted_iota(jnp.int32, (MOBA_BLOCK, MOBA_BLOCK), 0)
              <= lax.broadcasted_iota(jnp.int32, (MOBA_BLOCK, MOBA_BLOCK), 1))

    def last_tile(s_ref):
        for c in range(q_blocks):
            own = pl.ds(pl.multiple_of(((first_qb + c) % blocks_per_tile) * MOBA_BLOCK, MOBA_BLOCK),
                        MOBA_BLOCK)
            cols = slice(c * MOBA_BLOCK, (c + 1) * MOBA_BLOCK)
            s_ref[own, cols] = jnp.where(causal, s_ref[own, cols], NEG)
        s = s_ref[...]
        softmax_pv(n_past, s, jnp.max(s, axis=0, keepdims=True))

    @pl.when(n_past % 2 == 1)
    def _():
        scores(n_past, s1_ref, mx1_ref)
        softmax_pv(n_past - 1, s0_ref[...], mx0_ref[...])
        last_tile(s1_ref)

    @pl.when(n_past % 2 == 0)
    def _():
        last_tile(s0_ref)

    acc = acc_ref[...]
    out_t = acc[:ATTN_HEAD] / acc[ATTN_HEAD:ATTN_HEAD + 1]
    o_ref[...] = out_t.T.astype(o_ref.dtype)


def _moba_attention(q_aug, k, vt):
    heads, s, _ = q_aug.shape
    kt = _tile(s, ATTN_KEY_TILE)
    block_onehot = (jnp.arange(s, dtype=jnp.int32)[:, None] // MOBA_BLOCK
                    == jnp.arange(LANES, dtype=jnp.int32)[None, :]).astype(BF16)
    q_blocks = ATTN_Q_BLOCKS if (kt // MOBA_BLOCK) % ATTN_Q_BLOCKS == 0 else 1
    qw = q_blocks * MOBA_BLOCK
    return pl.pallas_call(
        functools.partial(_attn_kernel, kt=kt),
        grid=(heads, s // qw),
        in_specs=[
            pl.BlockSpec((1, qw, 2 * LANES), lambda h, i: (h, i, 0)),
            pl.BlockSpec((s, ATTN_HEAD), lambda h, i: (0, h)),
            pl.BlockSpec((s, LANES), lambda h, i: (0, 0)),
            pl.BlockSpec((1, VT_ROWS, s), lambda h, i: (h, 0, 0)),
        ],
        out_specs=pl.BlockSpec((qw, ATTN_HEAD), lambda h, i: (i, h)),
        out_shape=jax.ShapeDtypeStruct((s, heads * ATTN_HEAD), BF16),
        scratch_shapes=[
            pltpu.VMEM((kt, qw), F32), pltpu.VMEM((kt, qw), F32),
            pltpu.VMEM((1, qw), F32), pltpu.VMEM((1, qw), F32),
            pltpu.VMEM((1, qw), F32), pltpu.VMEM((VT_ROWS, qw), F32),
        ],
        compiler_params=_cparams(("parallel", "arbitrary")),
        name="moba_attention",
    )(q_aug, k, block_onehot, vt)


def _ffn_block(x, xb, w_gate, w_up, w_down, g, b, alpha, name):
    f = w_gate.shape[1]
    fp = _pad_to(f, 256)
    wg = jnp.pad(w_gate, ((0, 0), (0, fp - f))).astype(BF16)
    wu = jnp.pad(w_up, ((0, 0), (0, fp - f))).astype(BF16)
    wd = jnp.pad(w_down, ((0, fp - f), (0, 0))).astype(BF16)
    (h,) = _matmul([xb], [wg, wu], [(0, 0), (0, 1)], _swiglu_epilogue, [BF16],
                   tm=1024, tn=256, name=name + "_up")
    (y,) = _matmul([h], [wd], [(0, 0)], _identity_epilogue, [BF16],
                   tm=512, tn=512, name=name + "_down")
    return _residual_ln(x, y, g, b, alpha, 0.5, name + "_ln")


def _token_mixing(x1b, positions, w_in, shift_mix, decay_w0, decay_up, iclr_a0, iclr_up, gate_up,
                  k_k, k_a, r_k, lnx_w, lnx_b, w_o_rwkv, w_o_attn, w_out):
    s, d = x1b.shape
    c = decay_w0.shape[0]
    gate_rank = gate_up.shape[0]
    decay_rank, iclr_rank = decay_up.shape[0], iclr_up.shape[0]
    rwkv_in = 3 * c + decay_rank + iclr_rank + gate_rank
    cw = w_o_attn.shape[0]
    heads = cw // ATTN_HEAD

    lora_w = _pad_to(decay_rank + iclr_rank + gate_rank, 256)
    pad = lora_w - (decay_rank + iclr_rank + gate_rank)
    w_rwkv = jnp.pad(w_in[:, :rwkv_in].astype(BF16), ((0, 0), (0, pad)))
    w_attn = w_in[:, rwkv_in:].astype(BF16)
    mix_p = jnp.pad(shift_mix, (0, pad)).reshape(1, -1)
    gate_up_p = jnp.pad(gate_up, ((0, pad), (0, 0))).astype(BF16)

    (p_rwkv,) = _matmul([x1b], [w_rwkv], [(0, 0)], _identity_epilogue, [F32],
                        tm=2048, tn=256, name="in_proj_rwkv")
    (p_attn,) = _matmul([x1b], [w_attn], [(0, 0)], _identity_epilogue, [F32],
                        tm=2048, tn=512, name="in_proj_attn")

    r, lw, k_mod, v, a, b, g, bonus = _rwkv_prep(
        p_rwkv, c, lora_w, mix_p, decay_w0.reshape(1, c), iclr_a0.reshape(1, c), k_k.reshape(1, c),
        k_a.reshape(1, c), r_k.reshape(1, c), decay_up.astype(BF16), iclr_up.astype(BF16), gate_up_p)
    y = _rwkv_scan(r, lw, k_mod, v, a, b)
    yg = _rwkv_post(y, bonus, g, lnx_w.reshape(1, c), lnx_b.reshape(1, c))

    half = ATTN_HEAD // 2
    inv_half = ROPE_THETA ** (-jnp.arange(0, ATTN_HEAD, 2, dtype=F32) / ATTN_HEAD)
    inv = jnp.concatenate([inv_half, inv_half]).reshape(1, ATTN_HEAD)
    sign = jnp.concatenate([-jnp.ones((half,), F32), jnp.ones((half,), F32)]).reshape(1, ATTN_HEAD)
    pos_col = positions.astype(F32).reshape(s, 1)
    assert cw % (2 * LANES) == 0
    cos, sin = _rope_tables(pos_col, inv, sign)
    k_rope, v_b, kmean = _kv_prep(p_attn, cos, sin, cw, 2 * cw, cw)
    q_aug = _q_prep(p_attn, cos, sin, kmean, 0, heads)
    attn = _moba_attention(q_aug, k_rope, v_b)

    gate_off = 3 * cw // 256
    (merged,) = _matmul([yg, attn], [w_o_rwkv.astype(BF16), w_o_attn.astype(BF16)], [(0, 0), (1, 1)],
                        _gated_merge_epilogue, [BF16], tm=1024, tn=256,
                        extras=[(p_attn, gate_off), (p_attn, gate_off + d // 256)], name="branch_out")
    (mix,) = _matmul([merged], [w_out.astype(BF16)], [(0, 0)], _identity_epilogue, [BF16],
                     tm=2048, tn=512, name="mix_out")
    return mix


def kernel(x, positions, ffn1_w_gate, ffn1_w_up, ffn1_w_down, ln1_g, ln1_b, w_in, shift_mix, decay_w0, decay_up, iclr_a0, iclr_up, gate_up, k_k, k_a, r_k, lnx_w, lnx_b, w_o_rwkv, w_o_attn, w_out, ln2_g, ln2_b, ffn2_w_gate, ffn2_w_up, ffn2_w_down, ln3_g, ln3_b):
    bsz, s, d = x.shape
    depth = w_in.shape[0]
    alpha = (2 * depth) ** 0.25
    outs = []
    for bi in range(bsz):
        xf = x.reshape(s, d) if bsz == 1 else x[bi]
        xb = xf.astype(BF16)
        for l in range(depth):
            xf, xb = _ffn_block(xf, xb, ffn1_w_gate[l], ffn1_w_up[l], ffn1_w_down[l],
                                ln1_g[l], ln1_b[l], alpha, "ffn1")
            mix = _token_mixing(xb, positions[bi], w_in[l], shift_mix[l], decay_w0[l], decay_up[l],
                                iclr_a0[l], iclr_up[l], gate_up[l], k_k[l], k_a[l], r_k[l],
                                lnx_w[l], lnx_b[l], w_o_rwkv[l], w_o_attn[l], w_out[l])
            xf, xb = _residual_ln(xf, mix, ln2_g[l], ln2_b[l], alpha, 1.0, "ln2")
            xf, xb = _ffn_block(xf, xb, ffn2_w_gate[l], ffn2_w_up[l], ffn2_w_down[l],
                                ln3_g[l], ln3_b[l], alpha, "ffn2")
        outs.append(xf)
    return outs[0].reshape(1, s, d) if bsz == 1 else jnp.stack(outs)
```
